```python
import math
import jax
import jax.numpy as jnp
from jax import lax
import numpy as np

D_MODEL = 1024
BATCH = 8
SEQ = 4096
DEPTH = 1

PLE_DIM = 256
EPS = 1e-6

D_RNN = 1024
RNN_BLOCKS = 8
RNN_BLOCK = D_RNN // RNN_BLOCKS
CONV_W = 4
LRU_C = 8.0

N_HEADS = 8
N_KV = 2
HPG = N_HEADS // N_KV
HEAD_DIM = 128
D_ATTN = N_HEADS * HEAD_DIM
CMP_LEN = 32
CMP_STRIDE = 16
CMP_HID = 256
SEL_BLOCK = 64
SEL_TOPK = 16
WINDOW = 512
Q_BLOCK = 32
ROPE_THETA = 10000.0

PEER_HEADS = 8
N_KEYS = 128
N_EXPERTS = N_KEYS * N_KEYS
PEER_QDIM = 128
PEER_HALF = PEER_QDIM // 2
PEER_TOPK = 16
PEER_CHUNK = 128

SPLIT_SIZES = (D_RNN, D_RNN, D_ATTN, 6 * N_KV * HEAD_DIM, 3 * N_HEADS, 2 * D_MODEL)
D_IN = sum(SPLIT_SIZES)
SPLIT_POINTS = tuple(int(v) for v in np.cumsum(SPLIT_SIZES)[:-1])

NEG_INF = -1e30
FORCE_SCORE = 1e3

kernel_name = 'hybrid_rglru_nsa_peer'


def rmsnorm(x, g):
    xf = x.astype(jnp.float32)
    y = xf * lax.rsqrt(jnp.mean(xf * xf, axis=-1, keepdims=True) + EPS)
    return (y * g.astype(jnp.float32)).astype(x.dtype)


def rope(x, pos):
    half = HEAD_DIM // 2
    inv = ROPE_THETA ** (-jnp.arange(half, dtype=jnp.float32) / half)
    ang = pos.astype(jnp.float32)[..., None] * inv
    cos = jnp.cos(ang)[:, :, None, :]
    sin = jnp.sin(ang)[:, :, None, :]
    xf = x.astype(jnp.float32)
    x1, x2 = xf[..., :half], xf[..., half:]
    return jnp.concatenate([x1 * cos - x2 * sin, x2 * cos + x1 * sin], axis=-1).astype(x.dtype)


def causal_dwconv(x, w, b):
    y = lax.conv_general_dilated(x, w[:, None, :].astype(x.dtype), window_strides=(1,),
                                 padding=[(CONV_W - 1, 0)], dimension_numbers=('NWC', 'WIO', 'NWC'),
                                 feature_group_count=x.shape[-1])
    return y + b


def rg_lru(x, w_r, b_r, w_i, b_i, lam):
    B, S, C = x.shape
    xf = x.astype(jnp.float32)
    xb = xf.reshape(B, S, RNN_BLOCKS, RNN_BLOCK)
    r = jax.nn.sigmoid(jnp.einsum('bsnc,ncd->bsnd', xb, w_r.astype(jnp.float32)).reshape(B, S, C) + b_r)
    ig = jax.nn.sigmoid(jnp.einsum('bsnc,ncd->bsnd', xb, w_i.astype(jnp.float32)).reshape(B, S, C) + b_i)
    log_a = -LRU_C * r * jax.nn.softplus(-lam.astype(jnp.float32))
    a = jnp.exp(log_a)
    u = jnp.sqrt(-jnp.expm1(2.0 * log_a)) * (ig * xf)

    def combine(lhs, rhs):
        a1, b1 = lhs
        a2, b2 = rhs
        return a1 * a2, a2 * b1 + b2

    _, hs = lax.associative_scan(combine, (a, u), axis=1)
    return hs.astype(x.dtype)


def compress_blocks(t, idx, pe, w1, b1, w2, b2):
    blk = t[:, idx] + pe[:, None, :]
    B, n = blk.shape[0], blk.shape[1]
    flat = blk.transpose(0, 3, 1, 2, 4).reshape(B, N_KV, n, CMP_LEN * HEAD_DIM)
    hid = jax.nn.gelu(jnp.einsum('bgnf,fh->bgnh', flat, w1) + b1)
    return jnp.einsum('bgnh,hd->bgnd', hid, w2) + b2


def masked_softmax(s, mask):
    return jax.nn.softmax(jnp.where(mask, s, NEG_INF), axis=-1)


def nsa_attention(q, kv, gate_logits, pos, cmp_k, cmp_v):
    B, S = q.shape[0], q.shape[1]
    f32 = jnp.float32
    kv = kv.reshape(B, S, 6, N_KV, HEAD_DIM)
    k_c, v_c, k_s, v_s, k_w, v_w = (kv[:, :, j] for j in range(6))
    q = q.reshape(B, S, N_HEADS, HEAD_DIM)
    q_rot = rope(q, pos)
    k_s = rope(k_s, pos)
    k_w = rope(k_w, pos)

    def heads_first(t):
        return t.reshape(B, S, N_KV, HPG, HEAD_DIM).transpose(0, 2, 3, 1, 4)

    q_n = heads_first(q)
    q_r = heads_first(q_rot)
    gates = jax.nn.sigmoid(gate_logits.astype(f32)).reshape(B, S, N_KV, HPG, 3).transpose(0, 2, 3, 1, 4)

    n_cmp = (S - CMP_LEN) // CMP_STRIDE + 1
    cmp_start = jnp.arange(n_cmp) * CMP_STRIDE
    cmp_idx = cmp_start[:, None] + jnp.arange(CMP_LEN)[None, :]
    cmp_end = cmp_start + CMP_LEN - 1
    kc = compress_blocks(k_c, cmp_idx, *cmp_k)
    vc = compress_blocks(v_c, cmp_idx, *cmp_v)

    n_sel = S // SEL_BLOCK
    sel_start = jnp.arange(n_sel) * SEL_BLOCK
    overlap = ((cmp_start[:, None] < sel_start[None, :] + SEL_BLOCK) &
               (cmp_end[:, None] >= sel_start[None, :])).astype(f32)
    top_n = min(SEL_TOPK, n_sel)
    ks_blk = k_s.reshape(B, n_sel, SEL_BLOCK, N_KV, HEAD_DIM).transpose(0, 3, 1, 2, 4)
    vs_blk = v_s.reshape(B, n_sel, SEL_BLOCK, N_KV, HEAD_DIM).transpose(0, 3, 1, 2, 4)

    kw_pad = jnp.pad(k_w.transpose(0, 2, 1, 3), ((0, 0), (0, 0), (WINDOW, 0), (0, 0)))
    vw_pad = jnp.pad(v_w.transpose(0, 2, 1, 3), ((0, 0), (0, 0), (WINDOW, 0), (0, 0)))

    scale = HEAD_DIM ** -0.5
    b_ix = jnp.arange(B)[:, None, None, None]
    g_ix = jnp.arange(N_KV)[None, :, None, None]
    j_sel = jnp.arange(n_sel)

    def block(c):
        t0 = c * Q_BLOCK
        tq = t0 + jnp.arange(Q_BLOCK)
        qn = lax.dynamic_slice_in_dim(q_n, t0, Q_BLOCK, axis=3)
        qr = lax.dynamic_slice_in_dim(q_r, t0, Q_BLOCK, axis=3)
        gt = lax.dynamic_slice_in_dim(gates, t0, Q_BLOCK, axis=3)

        s_c = jnp.einsum('bghtd,bgnd->bghtn', qn, kc, preferred_element_type=f32) * scale
        m_c = cmp_end[None, :] <= tq[:, None]
        p_c = masked_softmax(s_c, m_c) * (tq >= CMP_LEN - 1).astype(f32)[:, None]
        o_c = jnp.einsum('bghtn,bgnd->bghtd', p_c.astype(vc.dtype), vc)

        imp = jnp.einsum('bghtn,nj->bgtj', p_c, overlap)
        cur = tq // SEL_BLOCK
        forced = (j_sel[None, :] == 0) | (j_sel[None, :] == cur[:, None]) | (j_sel[None, :] == cur[:, None] - 1)
        imp = jnp.where(forced, FORCE_SCORE, imp)
        imp = jnp.where(sel_start[None, :] <= tq[:, None], imp, NEG_INF)
        top_v, top_i = lax.top_k(imp, top_n)

        ks = ks_blk[b_ix, g_ix, top_i].reshape(B, N_KV, Q_BLOCK, top_n * SEL_BLOCK, HEAD_DIM)
        vs = vs_blk[b_ix, g_ix, top_i].reshape(B, N_KV, Q_BLOCK, top_n * SEL_BLOCK, HEAD_DIM)
        kpos = (top_i[..., None] * SEL_BLOCK + jnp.arange(SEL_BLOCK)).reshape(B, N_KV, Q_BLOCK, top_n * SEL_BLOCK)
        m_s = (kpos <= tq[None, None, :, None]) & jnp.repeat(top_v > 0.5 * NEG_INF, SEL_BLOCK, axis=-1)
        s_s = jnp.einsum('bghtd,bgtkd->bghtk', qr, ks, preferred_element_type=f32) * scale
        p_s = masked_softmax(s_s, m_s[:, :, None])
        o_s = jnp.einsum('bghtk,bgtkd->bghtd', p_s.astype(vs.dtype), vs)

        kw = lax.dynamic_slice_in_dim(kw_pad, t0, Q_BLOCK + WINDOW, axis=2)
        vw = lax.dynamic_slice_in_dim(vw_pad, t0, Q_BLOCK + WINDOW, axis=2)
        wpos = t0 - WINDOW + jnp.arange(Q_BLOCK + WINDOW)
        dist = tq[:, None] - wpos[None, :]
        m_w = (wpos[None, :] >= 0) & (dist >= 0) & (dist < WINDOW)
        s_w = jnp.einsum('bghtd,bgkd->bghtk', qr, kw, preferred_element_type=f32) * scale
        p_w = masked_softmax(s_w, m_w)
        o_w = jnp.einsum('bghtk,bgkd->bghtd', p_w.astype(vw.dtype), vw)

        return gt[..., 0:1] * o_c + gt[..., 1:2] * o_s + gt[..., 2:3] * o_w

    o = lax.map(block, jnp.arange(S // Q_BLOCK))
    return o.transpose(1, 0, 4, 2, 3, 5).reshape(B, S, D_ATTN)


def peer_ffn(x, w_q, sub_keys, u_tab, v_tab):
    B, S, D = x.shape
    T = B * S
    xt = x.reshape(T, D)
    q = jnp.einsum('td,de->te', xt, w_q).reshape(T, PEER_HEADS, 2, PEER_HALF)
    s = jnp.einsum('thcq,cnq->thcn', q, sub_keys, preferred_element_type=jnp.float32)
    v1, i1 = lax.top_k(s[:, :, 0], PEER_TOPK)
    v2, i2 = lax.top_k(s[:, :, 1], PEER_TOPK)
    cand = (v1[..., :, None] + v2[..., None, :]).reshape(T, PEER_HEADS, PEER_TOPK * PEER_TOPK)
    best, ci = lax.top_k(cand, PEER_TOPK)
    e_idx = (jnp.take_along_axis(i1, ci // PEER_TOPK, axis=-1) * N_KEYS +
             jnp.take_along_axis(i2, ci % PEER_TOPK, axis=-1))
    g = jax.nn.softmax(best, axis=-1).astype(x.dtype)
    n_chunks = T // PEER_CHUNK

    def chunk(args):
        xc, ec, gc = args
        u = u_tab[ec]
        act = jax.nn.gelu(jnp.einsum('cd,chkd->chk', xc, u)) * gc
        v = v_tab[ec]
        return jnp.einsum('chk,chkd->cd', act, v)

    out = lax.map(chunk, (xt.reshape(n_chunks, PEER_CHUNK, D),
                          e_idx.reshape(n_chunks, PEER_CHUNK, PEER_HEADS, PEER_TOPK),
                          g.reshape(n_chunks, PEER_CHUNK, PEER_HEADS, PEER_TOPK)))
    return out.reshape(B, S, D)


def setup_inputs(seed: int = 0) -> dict:
    key = jax.random.key(seed)
    keys = jax.random.split(key, 48)
    counter = [0]
    f32 = jnp.float32

    def nxt():
        k = keys[counter[0]]
        counter[0] += 1
        return k

    def nrm(shape, scale):
        return jax.random.normal(nxt(), shape, f32) * scale

    def gain(shape):
        return 1.0 + nrm(shape, 0.01)

    L = DEPTH
    x = nrm((BATCH, SEQ, D_MODEL), 1.0)
    p = nrm((DEPTH, BATCH, SEQ, PLE_DIM), 1.0)
    offsets = jax.random.randint(nxt(), (BATCH, 1), 0, 1024, dtype=jnp.int32)
    positions = offsets + jnp.arange(SEQ, dtype=jnp.int32)[None, :]
    u = jax.random.uniform(nxt(), (L, D_RNN), f32, 0.9, 0.999)
    s = u ** (1.0 / LRU_C)
    lru_lam = jnp.log(s) - jnp.log1p(-s)
    fl = CMP_LEN * HEAD_DIM
    return {
        'x': x,
        'p': p,
        'positions': positions,
        'norm_mix': gain((L, D_MODEL)),
        'w_in': nrm((L, D_MODEL, D_IN), D_MODEL ** -0.5),
        'conv_w': nrm((L, CONV_W, D_RNN), CONV_W ** -0.5),
        'conv_b': nrm((L, D_RNN), 0.01),
        'lru_wr': nrm((L, RNN_BLOCKS, RNN_BLOCK, RNN_BLOCK), RNN_BLOCK ** -0.5),
        'lru_br': nrm((L, D_RNN), 0.01),
        'lru_wi': nrm((L, RNN_BLOCKS, RNN_BLOCK, RNN_BLOCK), RNN_BLOCK ** -0.5),
        'lru_bi': nrm((L, D_RNN), 0.01),
        'lru_lam': lru_lam,
        'cmp_pe_k': nrm((L, CMP_LEN, HEAD_DIM), 0.02),
        'cmp_w1_k': nrm((L, fl, CMP_HID), fl ** -0.5),
        'cmp_b1_k': nrm((L, CMP_HID), 0.01),
        'cmp_w2_k': nrm((L, CMP_HID, HEAD_DIM), CMP_HID ** -0.5),
        'cmp_b2_k': nrm((L, HEAD_DIM), 0.01),
        'cmp_pe_v': nrm((L, CMP_LEN, HEAD_DIM), 0.02),
        'cmp_w1_v': nrm((L, fl, CMP_HID), fl ** -0.5),
        'cmp_b1_v': nrm((L, CMP_HID), 0.01),
        'cmp_w2_v': nrm((L, CMP_HID, HEAD_DIM), CMP_HID ** -0.5),
        'cmp_b2_v': nrm((L, HEAD_DIM), 0.01),
        'w_a': nrm((L, D_RNN, D_MODEL), D_RNN ** -0.5),
        'w_b': nrm((L, D_ATTN, D_MODEL), D_ATTN ** -0.5),
        'w_out': nrm((L, D_MODEL, D_MODEL), D_MODEL ** -0.5),
        'norm_ffn': gain((L, D_MODEL)),
        'peer_wq': nrm((L, D_MODEL, PEER_HEADS * PEER_QDIM), D_MODEL ** -0.5),
        'peer_keys': nrm((L, 2, N_KEYS, PEER_HALF), PEER_HALF ** -0.5),
        'peer_u': nrm((L, N_EXPERTS, D_MODEL), D_MODEL ** -0.5),
        'peer_v': nrm((L, N_EXPERTS, D_MODEL), PEER_HEADS ** -0.5),
        'norm_ple': gain((L, D_MODEL)),
        'ple_wg': nrm((L, D_MODEL, D_MODEL), D_MODEL ** -0.5),
        'ple_wp': nrm((L, PLE_DIM, D_MODEL), PLE_DIM ** -0.5),
        'norm_final': gain((D_MODEL,)),
    }


def reference(x, p, positions, norm_mix, w_in, conv_w, conv_b, lru_wr, lru_br, lru_wi, lru_bi, lru_lam,
              cmp_pe_k, cmp_w1_k, cmp_b1_k, cmp_w2_k, cmp_b2_k, cmp_pe_v, cmp_w1_v, cmp_b1_v, cmp_w2_v, cmp_b2_v,
              w_a, w_b, w_out, norm_ffn, peer_wq, peer_keys, peer_u, peer_v, norm_ple, ple_wg, ple_wp,
              norm_final):
    B, S, D = x.shape
    h = x
    for i in range(DEPTH):
        n = rmsnorm(h, norm_mix[i])
        z = jnp.einsum('bsd,de->bse', n, w_in[i])
        z_rx, z_rg, z_q, z_kv, z_ng, z_m = jnp.split(z, SPLIT_POINTS, axis=-1)

        xa = causal_dwconv(z_rx, conv_w[i], conv_b[i])
        ha = rg_lru(xa, lru_wr[i], lru_br[i], lru_wi[i], lru_bi[i], lru_lam[i])
        y_a = jnp.einsum('bsc,cd->bsd', ha * jax.nn.gelu(z_rg), w_a[i])

        o_b = nsa_attention(z_q, z_kv, z_ng, positions,
                            (cmp_pe_k[i], cmp_w1_k[i], cmp_b1_k[i], cmp_w2_k[i], cmp_b2_k[i]),
                            (cmp_pe_v[i], cmp_w1_v[i], cmp_b1_v[i], cmp_w2_v[i], cmp_b2_v[i]))
        y_b = jnp.einsum('bsc,cd->bsd', o_b.astype(h.dtype), w_b[i])

        g_m = jax.nn.sigmoid(z_m.astype(jnp.float32)).reshape(B, S, 2, D)
        merged = (g_m[:, :, 0] * y_a + g_m[:, :, 1] * y_b).astype(h.dtype)
        h = h + jnp.einsum('bsd,de->bse', merged, w_out[i])

        h = h + peer_ffn(rmsnorm(h, norm_ffn[i]), peer_wq[i], peer_keys[i], peer_u[i], peer_v[i])

        gate = jax.nn.sigmoid(jnp.einsum('bsd,de->bse', rmsnorm(h, norm_ple[i]), ple_wg[i]).astype(jnp.float32))
        h = h + (gate * jnp.einsum('bsk,kd->bsd', p[i], ple_wp[i])).astype(h.dtype)
    return rmsnorm(h, norm_final)
```

```python
import functools

import numpy as np
import jax
import jax.numpy as jnp
from jax import lax
from jax.experimental import pallas as pl
from jax.experimental.pallas import tpu as pltpu

f32 = jnp.float32
bf16 = jnp.bfloat16
i32 = jnp.int32

EPS = 1e-6
D_RNN = 1024
RNN_BLOCKS = 8
RNN_BLOCK = D_RNN // RNN_BLOCKS
CONV_W = 4
LRU_C = 8.0
N_HEADS = 8
N_KV = 2
HPG = N_HEADS // N_KV
HEAD_DIM = 128
D_ATTN = N_HEADS * HEAD_DIM
CMP_LEN = 32
CMP_STRIDE = 16
CMP_HID = 256
SEL_BLOCK = 64
SEL_TOPK = 16
WINDOW = 512
ROPE_THETA = 10000.0
PEER_HEADS = 8
N_KEYS = 128
PEER_QDIM = 128
PEER_HALF = PEER_QDIM // 2
PEER_TOPK = 16
NEG_INF = -1e30
FORCE_SCORE = 1e3
LOWEST = -3e38

LANES = 128
SUBLANES = 8
VMEM_LIMIT_BYTES = 56 * 1024 * 1024

C_RX, C_RG, C_Q, C_M, C_KV, C_NG = 0, 1024, 2048, 3072, 5120, 6656
NZ = 6912

NT_DIMS = (((1,), (1,)), ((), ()))
TN_DIMS = (((0,), (0,)), ((), ()))


def _params(*sem):
    return pltpu.CompilerParams(dimension_semantics=sem, vmem_limit_bytes=VMEM_LIMIT_BYTES)


def _rms(x, g):
    return x * lax.rsqrt(jnp.mean(x * x, axis=-1, keepdims=True) + EPS) * g


def _split_bf16(x):
    hi = x.astype(bf16)
    return hi, (x - hi.astype(f32)).astype(bf16)


def _inproj_body(x_ref, g_ref, w_ref, o_ref, n_ref):
    @pl.when(pl.program_id(1) == 0)
    def _():
        n_ref[...] = _rms(x_ref[...], g_ref[...]).astype(bf16)

    o_ref[...] = jnp.dot(n_ref[...], w_ref[...], preferred_element_type=f32)


def _inproj(x2, g, w):
    T, D = x2.shape
    tm, tn = min(1024, T), 768
    return pl.pallas_call(
        _inproj_body,
        grid=(T // tm, NZ // tn),
        in_specs=[pl.BlockSpec((tm, D), lambda i, j: (i, 0)),
                  pl.BlockSpec((1, D), lambda i, j: (0, 0)),
                  pl.BlockSpec((D, tn), lambda i, j: (0, j))],
        out_specs=pl.BlockSpec((tm, tn), lambda i, j: (i, j)),
        out_shape=jax.ShapeDtypeStruct((T, NZ), f32),
        scratch_shapes=[pltpu.VMEM((tm, D), bf16)],
        compiler_params=_params("parallel", "arbitrary"),
        name="inproj",
    )(x2, g, w)


def _lru_body(zx_ref, zg_ref, cw_ref, cb_ref, wri_ref, br_ref, bi_ref, lam_ref, o_ref,
              xs_ref, a_ref, u_ref, h_ref):
    ts = zx_ref.shape[0]

    @pl.when(pl.program_id(1) == 0)
    def _():
        xs_ref[0:SUBLANES, :] = jnp.zeros((SUBLANES, D_RNN), f32)
        h_ref[...] = jnp.zeros_like(h_ref)

    xs_ref[SUBLANES:, :] = zx_ref[...]
    xa = cb_ref[...]
    for k in range(CONV_W):
        off = SUBLANES - (CONV_W - 1) + k
        xa = xa + cw_ref[k:k + 1, :] * xs_ref[off:off + ts, :]
    xs_ref[0:SUBLANES, :] = xs_ref[ts:ts + SUBLANES, :]

    sp = jax.nn.softplus(-lam_ref[...])
    xb = xa.astype(bf16)
    for n in range(RNN_BLOCKS):
        sl = slice(n * RNN_BLOCK, (n + 1) * RNN_BLOCK)
        gates = jnp.dot(xb[:, sl], wri_ref[n], preferred_element_type=f32)
        r = jax.nn.sigmoid(gates[:, :RNN_BLOCK] + br_ref[:, sl])
        ig = jax.nn.sigmoid(gates[:, RNN_BLOCK:] + bi_ref[:, sl])
        log_a = -LRU_C * r * sp[:, sl]
        a = jnp.exp(log_a)
        u = jnp.sqrt(-jnp.tanh(log_a) * (a * a + 1.0)) * (ig * xa[:, sl])
        a_ref[:, sl] = a
        u_ref[:, sl] = u

    def step8(j, h):
        base = pl.multiple_of(j * SUBLANES, SUBLANES)
        for k in range(SUBLANES):
            h = a_ref[pl.ds(base + k, 1), :] * h + u_ref[pl.ds(base + k, 1), :]
            u_ref[pl.ds(base + k, 1), :] = h
        return h

    h_ref[0:1, :] = lax.fori_loop(0, ts // SUBLANES, step8, h_ref[0:1, :])
    o_ref[...] = (u_ref[...] * jax.nn.gelu(zg_ref[...])).astype(bf16)


def _lru(z, B, S, cw, cb, wri, br, bi, lam):
    ts = min(512, S)
    nS = S // ts
    vec = lambda: pl.BlockSpec((1, D_RNN), lambda b, i: (0, 0))
    return pl.pallas_call(
        _lru_body,
        grid=(B, nS),
        in_specs=[pl.BlockSpec((ts, D_RNN), lambda b, i: (b * nS + i, C_RX // D_RNN)),
                  pl.BlockSpec((ts, D_RNN), lambda b, i: (b * nS + i, C_RG // D_RNN)),
                  pl.BlockSpec((CONV_W, D_RNN), lambda b, i: (0, 0)), vec(),
                  pl.BlockSpec((RNN_BLOCKS, RNN_BLOCK, 2 * RNN_BLOCK), lambda b, i: (0, 0, 0)),
                  vec(), vec(), vec()],
        out_specs=pl.BlockSpec((ts, D_RNN), lambda b, i: (b * nS + i, 0)),
        out_shape=jax.ShapeDtypeStruct((B * S, D_RNN), bf16),
        scratch_shapes=[pltpu.VMEM((ts + SUBLANES, D_RNN), f32), pltpu.VMEM((ts, D_RNN), f32),
                        pltpu.VMEM((ts, D_RNN), f32), pltpu.VMEM((SUBLANES, D_RNN), f32)],
        compiler_params=_params("parallel", "arbitrary"),
        name="rglru",
    )(z, z, cw, cb, wri, br, bi, lam)


def _prep_body(pos_ref, inv_ref, sgn_ref, zq_ref, zc_ref, zs_ref, zw_ref,
               qn_ref, qr_ref, kc_ref, vc_ref, ks_ref, vs_ref, kw_ref, vw_ref):
    ang = pos_ref[...].astype(f32) * inv_ref[...]
    cs = jnp.cos(ang)
    sn = jnp.sin(ang) * sgn_ref[...]

    def rope(v):
        return v * cs + pltpu.roll(v, HEAD_DIM // 2, axis=1) * sn

    scale = HEAD_DIM ** -0.5
    for h in range(N_HEADS):
        sl = slice(h * HEAD_DIM, (h + 1) * HEAD_DIM)
        q = zq_ref[:, sl]
        qn_ref[:, sl] = (q * scale).astype(bf16)
        qr_ref[:, sl] = (rope(q) * scale).astype(bf16)
    for g in range(N_KV):
        ksl = slice(g * HEAD_DIM, (g + 1) * HEAD_DIM)
        vsl = slice((N_KV + g) * HEAD_DIM, (N_KV + g + 1) * HEAD_DIM)
        kc_ref[g] = zc_ref[:, ksl].astype(bf16)
        vc_ref[g] = zc_ref[:, vsl].astype(bf16)
        ks_ref[g] = rope(zs_ref[:, ksl]).astype(bf16)
        vs_ref[g] = zs_ref[:, vsl].astype(bf16)
        kw_ref[g] = rope(zw_ref[:, ksl]).astype(bf16)
        vw_ref[g] = zw_ref[:, vsl].astype(bf16)


def _prep(z, pos, inv2, sgn, B, S):
    T = B * S
    tp = min(256, S)
    nS = S // tp
    kvw = 2 * N_KV * HEAD_DIM
    row = lambda b, i: b * nS + i
    kv_out = pl.BlockSpec((None, N_KV, tp, HEAD_DIM), lambda b, i: (b, 0, i, 0))
    kv_shape = jax.ShapeDtypeStruct((B, N_KV, S, HEAD_DIM), bf16)
    q_shape = jax.ShapeDtypeStruct((T, D_ATTN), bf16)
    return pl.pallas_call(
        _prep_body,
        grid=(B, nS),
        in_specs=[pl.BlockSpec((tp, 1), lambda b, i: (row(b, i), 0)),
                  pl.BlockSpec((1, HEAD_DIM), lambda b, i: (0, 0)),
                  pl.BlockSpec((1, HEAD_DIM), lambda b, i: (0, 0)),
                  pl.BlockSpec((tp, D_ATTN), lambda b, i: (row(b, i), C_Q // D_ATTN)),
                  pl.BlockSpec((tp, kvw), lambda b, i: (row(b, i), C_KV // kvw)),
                  pl.BlockSpec((tp, kvw), lambda b, i: (row(b, i), C_KV // kvw + 1)),
                  pl.BlockSpec((tp, kvw), lambda b, i: (row(b, i), C_KV // kvw + 2))],
        out_specs=[pl.BlockSpec((tp, D_ATTN), lambda b, i: (row(b, i), 0)),
                   pl.BlockSpec((tp, D_ATTN), lambda b, i: (row(b, i), 0))] + [kv_out] * 6,
        out_shape=[q_shape, q_shape] + [kv_shape] * 6,
        compiler_params=_params("parallel", "parallel"),
        name="attn_prep",
    )(pos, inv2, sgn, z, z, z, z)


def _cmp_body(kf_ref, vf_ref, w1k_ref, pek_ref, b1k_ref, w2k_ref, b2k_ref,
              w1v_ref, pev_ref, b1v_ref, w2v_ref, b2v_ref, kc_ref, vc_ref):
    nc = kf_ref.shape[0]
    half = CMP_STRIDE * HEAD_DIM

    def mlp(f_ref, w1_ref, pe_ref, b1_ref, w2_ref, b2_ref):
        f = f_ref[...]
        first = jnp.dot(f, w1_ref[0:half, :], preferred_element_type=f32)
        second = jnp.dot(f, w1_ref[half:2 * half, :], preferred_element_type=f32)
        pew = jnp.dot(pe_ref[...], w1_ref[...], preferred_element_type=f32)[0:1]
        hid = jax.nn.gelu(first + pltpu.roll(second, nc - 1, axis=0) + pew + b1_ref[...])
        return jnp.dot(hid.astype(bf16), w2_ref[...], preferred_element_type=f32) + b2_ref[...]

    kc_ref[...] = mlp(kf_ref, w1k_ref, pek_ref, b1k_ref, w2k_ref, b2k_ref)
    vc_ref[...] = mlp(vf_ref, w1v_ref, pev_ref, b1v_ref, w2v_ref, b2v_ref)


def _compress(kflat, vflat, pk, pv):
    B, G, NC, F = kflat.shape
    fl = CMP_LEN * HEAD_DIM
    flat = pl.BlockSpec((None, None, NC, F), lambda b, g: (b, g, 0, 0))
    full = lambda shape: pl.BlockSpec(shape, lambda b, g: (0,) * len(shape))
    wspecs = [full((fl, CMP_HID)), full((SUBLANES, fl)), full((1, CMP_HID)),
              full((CMP_HID, HEAD_DIM)), full((1, HEAD_DIM))]
    out = pl.BlockSpec((None, None, NC, HEAD_DIM), lambda b, g: (b, g, 0, 0))
    shape = jax.ShapeDtypeStruct((B, G, NC, HEAD_DIM), f32)
    return pl.pallas_call(
        _cmp_body,
        grid=(B, G),
        in_specs=[flat, flat] + wspecs + wspecs,
        out_specs=[out, out],
        out_shape=[shape, shape],
        compiler_params=_params("parallel", "parallel"),
        name="compress",
    )(kflat, vflat, *pk, *pv)


def _softmax0(s):
    e = jnp.exp(s - jnp.max(s, axis=0, keepdims=True))
    return e * (1.0 / jnp.sum(e, axis=0, keepdims=True))


def _nsa_body(qn_ref, qr_ref, gz_ref, kc_ref, vc_ref, ks_ref, vs_ref, kw_ref, vw_ref, ovt_ref,
              o_ref, sel_ref, gt_ref, *, S, tq, top_n, ck):
    g = pl.program_id(1)
    t0 = pl.program_id(2) * tq
    NC, NS = S // CMP_STRIDE, S // SEL_BLOCK
    W4 = HPG * tq

    qn = jnp.concatenate([qn_ref[:, h * HEAD_DIM:(h + 1) * HEAD_DIM] for h in range(HPG)], axis=0)
    qr = jnp.concatenate([qr_ref[:, h * HEAD_DIM:(h + 1) * HEAD_DIM] for h in range(HPG)], axis=0)
    trow = t0 + lax.broadcasted_iota(i32, (1, tq), 1)
    tcol = jnp.concatenate([trow] * HPG, axis=1)

    sc = lax.dot_general(kc_ref[...].astype(bf16), qn, NT_DIMS, preferred_element_type=f32)
    cmp_end = lax.broadcasted_iota(i32, (NC, W4), 0) * CMP_STRIDE + (CMP_LEN - 1)
    pc = _softmax0(jnp.where(cmp_end <= tcol, sc, NEG_INF)) * (tcol >= CMP_LEN - 1).astype(f32)
    ocT = lax.dot_general(vc_ref[...].astype(bf16), pc.astype(bf16), TN_DIMS, preferred_element_type=f32)

    ps = pc[:, 0:tq]
    for h in range(1, HPG):
        ps = ps + pc[:, h * tq:(h + 1) * tq]
    ps_hi, ps_lo = _split_bf16(ps)
    imp = (jnp.dot(ovt_ref[...], ps_hi, preferred_element_type=f32)
           + jnp.dot(ovt_ref[...], ps_lo, preferred_element_type=f32))
    j = lax.broadcasted_iota(i32, (NS, tq), 0)
    t = t0 + lax.broadcasted_iota(i32, (NS, tq), 1)
    cur = t // SEL_BLOCK
    forced = (j == 0) | (j == cur) | (j == cur - 1)
    imp = jnp.where(forced, FORCE_SCORE, imp)
    imp = jnp.where(j * SEL_BLOCK <= t, imp, NEG_INF)
    sel = jnp.zeros((NS, tq), f32)
    for _ in range(top_n):
        m = jnp.max(imp, axis=0, keepdims=True)
        first = jnp.min(jnp.where(imp == m, j, NS), axis=0, keepdims=True)
        hit = j == first
        sel = jnp.where(hit & (m > 0.5 * NEG_INF), 1.0, sel)
        imp = jnp.where(hit, LOWEST, imp)
    sel_ref[...] = sel

    r64 = lax.broadcasted_iota(i32, (SEL_BLOCK, W4), 0)
    bpc = ck // SEL_BLOCK

    def chunk(c, carry):
        m, l, acc = carry
        base = pl.multiple_of(c * ck, ck)
        s = lax.dot_general(ks_ref[pl.ds(base, ck), :], qr, NT_DIMS, preferred_element_type=f32)
        parts = []
        for r in range(bpc):
            srow = sel_ref[pl.ds(c * bpc + r, 1), :]
            lim = jnp.where(srow > 0.5, trow, -1) - (base + SEL_BLOCK * r)
            lim = jnp.concatenate([lim] * HPG, axis=1)
            parts.append(jnp.where(r64 <= lim, s[SEL_BLOCK * r:SEL_BLOCK * (r + 1), :], NEG_INF))
        s = jnp.concatenate(parts, axis=0)
        m_new = jnp.maximum(m, jnp.max(s, axis=0, keepdims=True))
        alpha = jnp.exp(m - m_new)
        p = jnp.exp(s - m_new)
        l = alpha * l + jnp.sum(p, axis=0, keepdims=True)
        pv = lax.dot_general(vs_ref[pl.ds(base, ck), :], p.astype(bf16), TN_DIMS, preferred_element_type=f32)
        return m_new, l, alpha * acc + pv

    init = (jnp.full((1, W4), NEG_INF, f32), jnp.zeros((1, W4), f32), jnp.zeros((HEAD_DIM, W4), f32))
    _, l, acc = lax.fori_loop(0, (t0 + tq + ck - 1) // ck, chunk, init)
    osT = acc * (1.0 / l)

    wl = WINDOW + tq
    start = pl.multiple_of(jnp.maximum(t0 - WINDOW, 0), tq)
    sw = lax.dot_general(kw_ref[pl.ds(start, wl), :], qr, NT_DIMS, preferred_element_type=f32)
    dist = tcol - (start + lax.broadcasted_iota(i32, (wl, W4), 0))
    pw = _softmax0(jnp.where(dist >= 0, jnp.where(dist < WINDOW, sw, NEG_INF), NEG_INF))
    owT = lax.dot_general(vw_ref[pl.ds(start, wl), :], pw.astype(bf16), TN_DIMS, preferred_element_type=f32)

    gt_ref[...] = jax.nn.sigmoid(gz_ref[...]).T

    def gate(branch):
        return jnp.concatenate([gt_ref[pl.ds(g * (HPG * 3) + h * 3 + branch, 1), :] for h in range(HPG)], axis=1)

    oT = gate(0) * ocT + gate(1) * osT + gate(2) * owT
    for h in range(HPG):
        o_ref[:, h * HEAD_DIM:(h + 1) * HEAD_DIM] = oT[:, h * tq:(h + 1) * tq].T.astype(bf16)


def _nsa(qn, qr, z, kc, vc, ks, vs, kw, vw, ovt, B, S):
    tq, ck = 128, 512
    assert S % ck == 0 and S >= WINDOW + tq
    nS = S // tq
    NC, NS = S // CMP_STRIDE, S // SEL_BLOCK
    gw = HPG * HEAD_DIM
    qspec = pl.BlockSpec((tq, gw), lambda b, g, i: (b * nS + i, g))
    seq = lambda n: pl.BlockSpec((None, None, n, HEAD_DIM), lambda b, g, i: (b, g, 0, 0))
    body = functools.partial(_nsa_body, S=S, tq=tq, top_n=min(SEL_TOPK, NS), ck=ck)
    return pl.pallas_call(
        body,
        grid=(B, N_KV, nS),
        in_specs=[qspec, qspec,
                  pl.BlockSpec((tq, LANES), lambda b, g, i: (b * nS + i, C_NG // LANES)),
                  seq(NC), seq(NC), seq(S), seq(S), seq(S), seq(S),
                  pl.BlockSpec((NS, NC), lambda b, g, i: (0, 0))],
        out_specs=pl.BlockSpec((tq, gw), lambda b, g, i: (b * nS + i, g)),
        out_shape=jax.ShapeDtypeStruct((B * S, D_ATTN), bf16),
        scratch_shapes=[pltpu.VMEM((NS, tq), f32), pltpu.VMEM((LANES, tq), f32)],
        compiler_params=_params("parallel", "parallel", "arbitrary"),
        name="nsa",
    )(qn, qr, z, kc, vc, ks, vs, kw, vw, ovt)


def _extract_max(x, tag, invalid):
    m = jnp.max(x, axis=0, keepdims=True)
    first = jnp.min(jnp.where(x == m, tag, invalid), axis=0, keepdims=True)
    return m, first


def _post_body(ha_ref, ob_ref, zm0_ref, zm1_ref, x_ref, wa_ref, wb_ref, wo_ref, gf_ref, wq_ref, kbd_ref,
               h_ref, n_ref, e_ref, g_ref, st_ref, va_ref, ia_ref, et_ref, gt_ref):
    tm = x_ref.shape[0]
    K = PEER_TOPK
    ya = jnp.dot(ha_ref[...], wa_ref[...], preferred_element_type=f32)
    yb = jnp.dot(ob_ref[...], wb_ref[...], preferred_element_type=f32)
    merged = jax.nn.sigmoid(zm0_ref[...]) * ya + jax.nn.sigmoid(zm1_ref[...]) * yb
    h = x_ref[...] + jnp.dot(merged.astype(bf16), wo_ref[...], preferred_element_type=f32)
    h_ref[...] = h
    n = _rms(h, gf_ref[...]).astype(bf16)
    n_ref[...] = n
    q = jnp.dot(n, wq_ref[...], preferred_element_type=f32).astype(bf16)
    st_ref[...] = lax.dot_general(kbd_ref[...], q, NT_DIMS, preferred_element_type=f32)

    rows = lax.broadcasted_iota(i32, (N_KEYS, tm), 0)

    def top_half(hc, _):
        x = st_ref[pl.ds(pl.multiple_of(hc * N_KEYS, N_KEYS), N_KEYS), :]
        for k in range(K):
            m, first = _extract_max(x, rows, N_KEYS)
            va_ref[pl.ds(hc * K + k, 1), :] = m
            ia_ref[pl.ds(hc * K + k, 1), :] = first
            x = jnp.where(rows == first, LOWEST, x)
        return 0

    lax.fori_loop(0, 2 * PEER_HEADS, top_half, 0)

    r8 = lax.broadcasted_iota(i32, (SUBLANES, tm), 0)

    def combine(hd, _):
        b1 = pl.multiple_of(hd * 2 * K, 2 * K)
        v1, v2 = va_ref[pl.ds(b1, K), :], va_ref[pl.ds(b1 + K, K), :]
        i1, i2 = ia_ref[pl.ds(b1, K), :], ia_ref[pl.ds(b1 + K, K), :]
        cand = [v1[0:1] + v2]
        tag = [lax.broadcasted_iota(i32, (K, tm), 0)]
        eid = [i1[0:1] * N_KEYS + i2]
        for a in range(1, SUBLANES):
            ok = r8 < K // (a + 1)
            cand.append(jnp.where(ok, v1[a:a + 1] + v2[0:SUBLANES], LOWEST))
            tag.append(a * K + r8)
            eid.append(i1[a:a + 1] * N_KEYS + i2[0:SUBLANES])
        cand.append(v1[SUBLANES:K] + v2[0:1])
        tag.append((SUBLANES + r8) * K)
        eid.append(i1[SUBLANES:K] * N_KEYS + i2[0:1])
        cand, tag, eid = (jnp.concatenate(v, axis=0) for v in (cand, tag, eid))
        best0 = None
        den = jnp.zeros((1, tm), f32)
        for k in range(K):
            m, first = _extract_max(cand, tag, K * K)
            hit = tag == first
            et_ref[pl.ds(hd * K + k, 1), :] = jnp.sum(jnp.where(hit, eid, 0), axis=0, keepdims=True)
            cand = jnp.where(hit, LOWEST, cand)
            best0 = m if best0 is None else best0
            ex = jnp.exp(m - best0)
            gt_ref[pl.ds(hd * K + k, 1), :] = ex
            den = den + ex
        gt_ref[pl.ds(hd * K, K), :] = gt_ref[pl.ds(hd * K, K), :] * (1.0 / den)
        return 0

    lax.fori_loop(0, PEER_HEADS, combine, 0)
    e_ref[...] = et_ref[...].T
    g_ref[...] = gt_ref[...].T


def _post(hag, ob, z, x2, wa, wb, wo, gf, wq, kbd):
    T, D = x2.shape
    tm = min(256, T)
    HK = PEER_HEADS * PEER_TOPK
    rowblk = lambda w, c=0: pl.BlockSpec((tm, w), lambda i: (i, c))
    full = lambda shape: pl.BlockSpec(shape, lambda i: (0,) * len(shape))
    return pl.pallas_call(
        _post_body,
        grid=(T // tm,),
        in_specs=[rowblk(D), rowblk(D), rowblk(D, C_M // D), rowblk(D, C_M // D + 1), rowblk(D),
                  full((D, D)), full((D, D)), full((D, D)), full((1, D)), full((D, D)),
                  full((2 * PEER_HEADS * N_KEYS, D))],
        out_specs=[rowblk(D), rowblk(D), rowblk(HK), rowblk(HK)],
        out_shape=[jax.ShapeDtypeStruct((T, D), f32), jax.ShapeDtypeStruct((T, D), bf16),
                   jax.ShapeDtypeStruct((T, HK), i32), jax.ShapeDtypeStruct((T, HK), f32)],
        scratch_shapes=[pltpu.VMEM((2 * PEER_HEADS * N_KEYS, tm), f32),
                        pltpu.VMEM((2 * PEER_HEADS * PEER_TOPK, tm), f32),
                        pltpu.VMEM((2 * PEER_HEADS * PEER_TOPK, tm), i32),
                        pltpu.VMEM((HK, tm), i32), pltpu.VMEM((HK, tm), f32)],
        compiler_params=_params("parallel"),
        name="post_route",
    )(hag, ob, z, z, x2, wa, wb, wo, gf, wq, kbd)


PEER_GROUP = 8


def _diag_mask(width):
    lane = lax.broadcasted_iota(i32, (SUBLANES, width), 1)
    sub = lax.broadcasted_iota(i32, (SUBLANES, width), 0)
    return (lane & (SUBLANES - 1)) == sub


def _gather_tiles(idx_ref, tab_ref, t):
    return jnp.concatenate([tab_ref[idx_ref[t, j]] for j in range(idx_ref.shape[1])], axis=0)


def _peer_u_body(idx_ref, x_ref, g_ref, tab_ref, fold_ref, act_ref, y_ref):
    tt, hk = g_ref.shape
    diag = _diag_mask(hk * SUBLANES)

    def group(gi, _):
        for k in range(PEER_GROUP):
            t = gi * PEER_GROUP + k
            y = lax.dot_general(x_ref[t], _gather_tiles(idx_ref, tab_ref, t), NT_DIMS, preferred_element_type=f32)
            y_ref[pl.ds(t, 1), :] = jnp.sum(jnp.where(diag, y, 0.0), axis=0, keepdims=True)
        return 0

    lax.fori_loop(0, tt // PEER_GROUP, group, 0)
    y_hi, y_lo = _split_bf16(y_ref[...])
    s = (jnp.dot(y_hi, fold_ref[...], preferred_element_type=f32)
         + jnp.dot(y_lo, fold_ref[...], preferred_element_type=f32))
    act_ref[...] = jax.nn.gelu(s) * g_ref[...]


def _peer_v_body(idx_ref, act_ref, h_ref, tab_ref, spread_ref, o_ref, a_ref):
    tt, hk = act_ref.shape
    diag = _diag_mask(hk * SUBLANES)
    a_hi, a_lo = _split_bf16(act_ref[...])
    a_ref[...] = (jnp.dot(a_hi, spread_ref[...], preferred_element_type=f32)
                  + jnp.dot(a_lo, spread_ref[...], preferred_element_type=f32))

    def group(gi, _):
        for k in range(PEER_GROUP):
            t = gi * PEER_GROUP + k
            a8 = jnp.where(diag, jnp.broadcast_to(a_ref[pl.ds(t, 1), :], diag.shape), 0.0)
            lhs = jnp.concatenate(_split_bf16(a8), axis=0)
            out = jnp.dot(lhs, _gather_tiles(idx_ref, tab_ref, t), preferred_element_type=f32)
            o_ref[t] = h_ref[t] + out[0:SUBLANES] + out[SUBLANES:2 * SUBLANES]
        return 0

    lax.fori_loop(0, tt // PEER_GROUP, group, 0)


def _peer(n3, e_idx, gates, h3, u_tiles, v_tiles, fold, spread):
    T = n3.shape[0]
    HK = e_idx.shape[1]
    tt = min(64, T)
    smem_idx = pl.BlockSpec((tt, HK), lambda i: (i, 0), memory_space=pltpu.SMEM)
    tok = pl.BlockSpec((tt, SUBLANES, LANES), lambda i: (i, 0, 0))
    row = pl.BlockSpec((tt, HK), lambda i: (i, 0))
    table = pl.BlockSpec(memory_space=pltpu.VMEM)
    act = pl.pallas_call(
        _peer_u_body,
        grid=(T // tt,),
        in_specs=[smem_idx, tok, row, table, pl.BlockSpec(fold.shape, lambda i: (0, 0))],
        out_specs=row,
        out_shape=jax.ShapeDtypeStruct((T, HK), f32),
        scratch_shapes=[pltpu.VMEM((tt, HK * SUBLANES), f32)],
        compiler_params=_params("parallel"),
        name="peer_u",
    )(e_idx, n3, gates, u_tiles, fold)
    return pl.pallas_call(
        _peer_v_body,
        grid=(T // tt,),
        in_specs=[smem_idx, row, tok, table, pl.BlockSpec(spread.shape, lambda i: (0, 0))],
        out_specs=tok,
        out_shape=jax.ShapeDtypeStruct(h3.shape, f32),
        scratch_shapes=[pltpu.VMEM((tt, HK * SUBLANES), f32)],
        compiler_params=_params("parallel"),
        name="peer_v",
    )(e_idx, act, h3, v_tiles, spread)


def _ple_body(h_ref, p_ref, gp_ref, wg_ref, wp_ref, gn_ref, o_ref, *, last):
    h = h_ref[...]
    gate = jax.nn.sigmoid(jnp.dot(_rms(h, gp_ref[...]).astype(bf16), wg_ref[...], preferred_element_type=f32))
    h = h + gate * jnp.dot(p_ref[...].astype(bf16), wp_ref[...], preferred_element_type=f32)
    o_ref[...] = _rms(h, gn_ref[...]) if last else h


def _ple(h2, p2, gp, wg, wp, gn, last):
    T, D = h2.shape
    P = p2.shape[1]
    tm = min(512, T)
    full = lambda shape: pl.BlockSpec(shape, lambda i: (0,) * len(shape))
    return pl.pallas_call(
        functools.partial(_ple_body, last=last),
        grid=(T // tm,),
        in_specs=[pl.BlockSpec((tm, D), lambda i: (i, 0)), pl.BlockSpec((tm, P), lambda i: (i, 0)),
                  full((1, D)), full((D, D)), full((P, D)), full((1, D))],
        out_specs=pl.BlockSpec((tm, D), lambda i: (i, 0)),
        out_shape=jax.ShapeDtypeStruct((T, D), f32),
        compiler_params=_params("parallel"),
        name="ple_norm",
    )(h2, p2, gp, wg, wp, gn)


def _overlap_t(S):
    NC, NS = S // CMP_STRIDE, S // SEL_BLOCK
    n = np.arange(NC)[None, :] * CMP_STRIDE
    j = np.arange(NS)[:, None] * SEL_BLOCK
    ov = (n < j + SEL_BLOCK) & (n + CMP_LEN - 1 >= j) & (np.arange(NC)[None, :] < NC - 1)
    return jnp.asarray(ov, bf16)


def _fold_matrix(hk):
    return jnp.asarray(np.arange(hk * SUBLANES)[:, None] // SUBLANES == np.arange(hk)[None, :], bf16)


def kernel(x, p, positions, norm_mix, w_in, conv_w, conv_b, lru_wr, lru_br, lru_wi, lru_bi, lru_lam,
           cmp_pe_k, cmp_w1_k, cmp_b1_k, cmp_w2_k, cmp_b2_k, cmp_pe_v, cmp_w1_v, cmp_b1_v, cmp_w2_v, cmp_b2_v,
           w_a, w_b, w_out, norm_ffn, peer_wq, peer_keys, peer_u, peer_v, norm_ple, ple_wg, ple_wp,
           norm_final):
    B, S, D = x.shape
    T = B * S
    depth = norm_mix.shape[0]
    HK = PEER_HEADS * PEER_TOPK
    half = HEAD_DIM // 2
    inv = ROPE_THETA ** (-jnp.arange(half, dtype=f32) / half)
    inv2 = jnp.concatenate([inv, inv])[None, :]
    sgn = jnp.concatenate([-jnp.ones((half,), f32), jnp.ones((half,), f32)])[None, :]
    pos = positions.reshape(T, 1)
    ovt = _overlap_t(S)
    fold = _fold_matrix(HK)
    spread = fold.T
    fl = CMP_LEN * HEAD_DIM
    row = lambda v: v.reshape(1, -1)

    h = x.reshape(T, D)
    for i in range(depth):
        w = w_in[i]
        sp = np.cumsum((D_RNN, D_RNN, D_ATTN, 6 * N_KV * HEAD_DIM, 3 * N_HEADS, 2 * D))
        w_ng = jnp.pad(w[:, sp[3]:sp[4]], ((0, 0), (0, NZ - C_NG - 3 * N_HEADS)))
        w_all = jnp.concatenate([w[:, :sp[2]], w[:, sp[4]:], w[:, sp[2]:sp[3]], w_ng], axis=1).astype(bf16)
        z = _inproj(h, row(norm_mix[i]), w_all)

        wri = jnp.concatenate([lru_wr[i], lru_wi[i]], axis=-1).astype(bf16)
        hag = _lru(z, B, S, conv_w[i], row(conv_b[i]), wri, row(lru_br[i]), row(lru_bi[i]), row(lru_lam[i]))

        qn, qr, kc_in, vc_in, ks, vs, kw, vw = _prep(z, pos, inv2, sgn, B, S)
        flat = lambda a: a.reshape(B, N_KV, S // CMP_STRIDE, CMP_STRIDE * HEAD_DIM)
        cmp_params = lambda pe, w1, b1, w2, b2: (
            w1.astype(bf16), jnp.broadcast_to(pe.reshape(1, fl), (SUBLANES, fl)).astype(bf16), row(b1),
            w2.astype(bf16), row(b2))
        kc, vc = _compress(flat(kc_in), flat(vc_in),
                           cmp_params(cmp_pe_k[i], cmp_w1_k[i], cmp_b1_k[i], cmp_w2_k[i], cmp_b2_k[i]),
                           cmp_params(cmp_pe_v[i], cmp_w1_v[i], cmp_b1_v[i], cmp_w2_v[i], cmp_b2_v[i]))
        ob = _nsa(qn, qr, z, kc, vc, ks, vs, kw, vw, ovt, B, S)

        keys = peer_keys[i]
        kbd = jnp.zeros((PEER_HEADS, 2, N_KEYS, PEER_HEADS, 2, PEER_HALF), f32)
        for hd in range(PEER_HEADS):
            for c in range(2):
                kbd = kbd.at[hd, c, :, hd, c, :].set(keys[c])
        kbd = kbd.reshape(2 * PEER_HEADS * N_KEYS, PEER_HEADS * PEER_QDIM).astype(bf16)
        h1, n2, e_idx, gates = _post(hag, ob, z, h, w_a[i].astype(bf16), w_b[i].astype(bf16),
                                     w_out[i].astype(bf16), row(norm_ffn[i]), peer_wq[i].astype(bf16), kbd)

        tiles = lambda tab: tab.astype(bf16).reshape(tab.shape[0], SUBLANES, LANES)
        h2 = _peer(n2.reshape(T, SUBLANES, LANES), e_idx, gates, h1.reshape(T, SUBLANES, LANES),
                   tiles(peer_u[i]), tiles(peer_v[i]), fold, spread)

        h = _ple(h2.reshape(T, D), p[i].reshape(T, -1), row(norm_ple[i]), ple_wg[i].astype(bf16),
                 ple_wp[i].astype(bf16), row(norm_final), last=(i == depth - 1))
    return h.reshape(B, S, D)
```

```python
import functools

import numpy as np
import jax
import jax.numpy as jnp
from jax import lax
from jax.experimental import pallas as pl
from jax.experimental.pallas import tpu as pltpu

f32 = jnp.float32
bf16 = jnp.bfloat16
i32 = jnp.int32

EPS = 1e-6
D_RNN = 1024
RNN_BLOCKS = 8
RNN_BLOCK = D_RNN // RNN_BLOCKS
CONV_W = 4
LRU_C = 8.0
N_HEADS = 8
N_KV = 2
HPG = N_HEADS // N_KV
HEAD_DIM = 128
D_ATTN = N_HEADS * HEAD_DIM
CMP_LEN = 32
CMP_STRIDE = 16
CMP_HID = 256
SEL_BLOCK = 64
SEL_TOPK = 16
WINDOW = 512
ROPE_THETA = 10000.0
PEER_HEADS = 8
N_KEYS = 128
PEER_QDIM = 128
PEER_HALF = PEER_QDIM // 2
PEER_TOPK = 16
NEG_INF = -1e30
FORCE_SCORE = 1e3
LOWEST = -3e38

LANES = 128
SUBLANES = 8
VMEM_LIMIT_BYTES = 56 * 1024 * 1024

C_RX, C_RG, C_Q, C_M, C_KV, C_NG = 0, 1024, 2048, 3072, 5120, 6656
NZ = 6912

NT_DIMS = (((1,), (1,)), ((), ()))
TN_DIMS = (((0,), (0,)), ((), ()))


def _params(*sem):
    return pltpu.CompilerParams(dimension_semantics=sem, vmem_limit_bytes=VMEM_LIMIT_BYTES)


def _rms(x, g):
    return x * lax.rsqrt(jnp.mean(x * x, axis=-1, keepdims=True) + EPS) * g


def _split_bf16(x):
    hi = x.astype(bf16)
    return hi, (x - hi.astype(f32)).astype(bf16)


def _inproj_body(x_ref, g_ref, w_ref, o_ref, n_ref):
    @pl.when(pl.program_id(1) == 0)
    def _():
        n_ref[...] = _rms(x_ref[...], g_ref[...]).astype(bf16)

    o_ref[...] = jnp.dot(n_ref[...], w_ref[...], preferred_element_type=f32)


def _inproj(x2, g, w):
    T, D = x2.shape
    tm, tn = min(1024, T), 768
    return pl.pallas_call(
        _inproj_body,
        grid=(T // tm, NZ // tn),
        in_specs=[pl.BlockSpec((tm, D), lambda i, j: (i, 0)),
                  pl.BlockSpec((1, D), lambda i, j: (0, 0)),
                  pl.BlockSpec((D, tn), lambda i, j: (0, j))],
        out_specs=pl.BlockSpec((tm, tn), lambda i, j: (i, j)),
        out_shape=jax.ShapeDtypeStruct((T, NZ), f32),
        scratch_shapes=[pltpu.VMEM((tm, D), bf16)],
        compiler_params=_params("parallel", "arbitrary"),
        name="inproj",
    )(x2, g, w)


def _lru_body(zx_ref, zg_ref, cw_ref, cb_ref, wri_ref, br_ref, bi_ref, lam_ref, o_ref,
              xs_ref, a_ref, u_ref, h_ref):
    ts = zx_ref.shape[0]

    @pl.when(pl.program_id(1) == 0)
    def _():
        xs_ref[0:SUBLANES, :] = jnp.zeros((SUBLANES, D_RNN), f32)
        h_ref[...] = jnp.zeros_like(h_ref)

    xs_ref[SUBLANES:, :] = zx_ref[...]
    xa = cb_ref[...]
    for k in range(CONV_W):
        off = SUBLANES - (CONV_W - 1) + k
        xa = xa + cw_ref[k:k + 1, :] * xs_ref[off:off + ts, :]
    xs_ref[0:SUBLANES, :] = xs_ref[ts:ts + SUBLANES, :]

    sp = jax.nn.softplus(-lam_ref[...])
    xb = xa.astype(bf16)
    for n in range(RNN_BLOCKS):
        sl = slice(n * RNN_BLOCK, (n + 1) * RNN_BLOCK)
        gates = jnp.dot(xb[:, sl], wri_ref[n], preferred_element_type=f32)
        r = jax.nn.sigmoid(gates[:, :RNN_BLOCK] + br_ref[:, sl])
        ig = jax.nn.sigmoid(gates[:, RNN_BLOCK:] + bi_ref[:, sl])
        log_a = -LRU_C * r * sp[:, sl]
        a = jnp.exp(log_a)
        u = jnp.sqrt(-jnp.tanh(log_a) * (a * a + 1.0)) * (ig * xa[:, sl])
        a_ref[:, sl] = a
        u_ref[:, sl] = u

    def step8(j, h):
        base = pl.multiple_of(j * SUBLANES, SUBLANES)
        for k in range(SUBLANES):
            h = a_ref[pl.ds(base + k, 1), :] * h + u_ref[pl.ds(base + k, 1), :]
            u_ref[pl.ds(base + k, 1), :] = h
        return h

    h_ref[0:1, :] = lax.fori_loop(0, ts // SUBLANES, step8, h_ref[0:1, :])
    o_ref[...] = (u_ref[...] * jax.nn.gelu(zg_ref[...])).astype(bf16)


def _lru(z, B, S, cw, cb, wri, br, bi, lam):
    ts = min(512, S)
    nS = S // ts
    vec = lambda: pl.BlockSpec((1, D_RNN), lambda b, i: (0, 0))
    return pl.pallas_call(
        _lru_body,
        grid=(B, nS),
        in_specs=[pl.BlockSpec((ts, D_RNN), lambda b, i: (b * nS + i, C_RX // D_RNN)),
                  pl.BlockSpec((ts, D_RNN), lambda b, i: (b * nS + i, C_RG // D_RNN)),
                  pl.BlockSpec((CONV_W, D_RNN), lambda b, i: (0, 0)), vec(),
                  pl.BlockSpec((RNN_BLOCKS, RNN_BLOCK, 2 * RNN_BLOCK), lambda b, i: (0, 0, 0)),
                  vec(), vec(), vec()],
        out_specs=pl.BlockSpec((ts, D_RNN), lambda b, i: (b * nS + i, 0)),
        out_shape=jax.ShapeDtypeStruct((B * S, D_RNN), bf16),
        scratch_shapes=[pltpu.VMEM((ts + SUBLANES, D_RNN), f32), pltpu.VMEM((ts, D_RNN), f32),
                        pltpu.VMEM((ts, D_RNN), f32), pltpu.VMEM((SUBLANES, D_RNN), f32)],
        compiler_params=_params("parallel", "arbitrary"),
        name="rglru",
    )(z, z, cw, cb, wri, br, bi, lam)


def _prep_body(pos_ref, inv_ref, sgn_ref, zq_ref, zc_ref, zs_ref, zw_ref,
               qn_ref, qr_ref, kc_ref, vc_ref, ks_ref, vs_ref, kw_ref, vw_ref):
    ang = pos_ref[...].astype(f32) * inv_ref[...]
    cs = jnp.cos(ang)
    sn = jnp.sin(ang) * sgn_ref[...]

    def rope(v):
        return v * cs + pltpu.roll(v, HEAD_DIM // 2, axis=1) * sn

    scale = HEAD_DIM ** -0.5
    for h in range(N_HEADS):
        sl = slice(h * HEAD_DIM, (h + 1) * HEAD_DIM)
        q = zq_ref[:, sl]
        qn_ref[:, sl] = (q * scale).astype(bf16)
        qr_ref[:, sl] = (rope(q) * scale).astype(bf16)
    for g in range(N_KV):
        ksl = slice(g * HEAD_DIM, (g + 1) * HEAD_DIM)
        vsl = slice((N_KV + g) * HEAD_DIM, (N_KV + g + 1) * HEAD_DIM)
        kc_ref[g] = zc_ref[:, ksl].astype(bf16)
        vc_ref[g] = zc_ref[:, vsl].astype(bf16)
        ks_ref[g] = rope(zs_ref[:, ksl]).astype(bf16)
        vs_ref[g] = zs_ref[:, vsl].astype(bf16)
        kw_ref[g] = rope(zw_ref[:, ksl]).astype(bf16)
        vw_ref[g] = zw_ref[:, vsl].astype(bf16)


def _prep(z, pos, inv2, sgn, B, S):
    T = B * S
    tp = min(256, S)
    nS = S // tp
    kvw = 2 * N_KV * HEAD_DIM
    row = lambda b, i: b * nS + i
    kv_out = pl.BlockSpec((None, N_KV, tp, HEAD_DIM), lambda b, i: (b, 0, i, 0))
    kv_shape = jax.ShapeDtypeStruct((B, N_KV, S, HEAD_DIM), bf16)
    q_shape = jax.ShapeDtypeStruct((T, D_ATTN), bf16)
    return pl.pallas_call(
        _prep_body,
        grid=(B, nS),
        in_specs=[pl.BlockSpec((tp, 1), lambda b, i: (row(b, i), 0)),
                  pl.BlockSpec((1, HEAD_DIM), lambda b, i: (0, 0)),
                  pl.BlockSpec((1, HEAD_DIM), lambda b, i: (0, 0)),
                  pl.BlockSpec((tp, D_ATTN), lambda b, i: (row(b, i), C_Q // D_ATTN)),
                  pl.BlockSpec((tp, kvw), lambda b, i: (row(b, i), C_KV // kvw)),
                  pl.BlockSpec((tp, kvw), lambda b, i: (row(b, i), C_KV // kvw + 1)),
                  pl.BlockSpec((tp, kvw), lambda b, i: (row(b, i), C_KV // kvw + 2))],
        out_specs=[pl.BlockSpec((tp, D_ATTN), lambda b, i: (row(b, i), 0)),
                   pl.BlockSpec((tp, D_ATTN), lambda b, i: (row(b, i), 0))] + [kv_out] * 6,
        out_shape=[q_shape, q_shape] + [kv_shape] * 6,
        compiler_params=_params("parallel", "parallel"),
        name="attn_prep",
    )(pos, inv2, sgn, z, z, z, z)


def _cmp_body(kf_ref, vf_ref, w1k_ref, pek_ref, b1k_ref, w2k_ref, b2k_ref,
              w1v_ref, pev_ref, b1v_ref, w2v_ref, b2v_ref, kc_ref, vc_ref):
    nc = kf_ref.shape[0]
    half = CMP_STRIDE * HEAD_DIM

    def mlp(f_ref, w1_ref, pe_ref, b1_ref, w2_ref, b2_ref):
        f = f_ref[...]
        first = jnp.dot(f, w1_ref[0:half, :], preferred_element_type=f32)
        second = jnp.dot(f, w1_ref[half:2 * half, :], preferred_element_type=f32)
        pew = jnp.dot(pe_ref[...], w1_ref[...], preferred_element_type=f32)[0:1]
        hid = jax.nn.gelu(first + pltpu.roll(second, nc - 1, axis=0) + pew + b1_ref[...])
        return jnp.dot(hid.astype(bf16), w2_ref[...], preferred_element_type=f32) + b2_ref[...]

    kc_ref[...] = mlp(kf_ref, w1k_ref, pek_ref, b1k_ref, w2k_ref, b2k_ref)
    vc_ref[...] = mlp(vf_ref, w1v_ref, pev_ref, b1v_ref, w2v_ref, b2v_ref)


def _compress(kflat, vflat, pk, pv):
    B, G, NC, F = kflat.shape
    fl = CMP_LEN * HEAD_DIM
    flat = pl.BlockSpec((None, None, NC, F), lambda b, g: (b, g, 0, 0))
    full = lambda shape: pl.BlockSpec(shape, lambda b, g: (0,) * len(shape))
    wspecs = [full((fl, CMP_HID)), full((SUBLANES, fl)), full((1, CMP_HID)),
              full((CMP_HID, HEAD_DIM)), full((1, HEAD_DIM))]
    out = pl.BlockSpec((None, None, NC, HEAD_DIM), lambda b, g: (b, g, 0, 0))
    shape = jax.ShapeDtypeStruct((B, G, NC, HEAD_DIM), f32)
    return pl.pallas_call(
        _cmp_body,
        grid=(B, G),
        in_specs=[flat, flat] + wspecs + wspecs,
        out_specs=[out, out],
        out_shape=[shape, shape],
        compiler_params=_params("parallel", "parallel"),
        name="compress",
    )(kflat, vflat, *pk, *pv)


def _softmax0(s):
    e = jnp.exp(s - jnp.max(s, axis=0, keepdims=True))
    return e * (1.0 / jnp.sum(e, axis=0, keepdims=True))


def _nsa_body(qn_ref, qr_ref, gz_ref, kc_ref, vc_ref, ks_ref, vs_ref, kw_ref, vw_ref, ovt_ref,
              o_ref, sel_ref, gt_ref, qrs_ref, part_ref, m_ref, l_ref, acc_ref, sa_ref, sb_ref,
              *, S, tq, top_n, ck):
    g = pl.program_id(1)
    t0 = pl.program_id(2) * tq
    NC, NS = S // CMP_STRIDE, S // SEL_BLOCK
    heads = [slice(h * HEAD_DIM, (h + 1) * HEAD_DIM) for h in range(HPG)]
    trow = t0 + lax.broadcasted_iota(i32, (1, tq), 1)

    tile4 = lambda a: jnp.concatenate([a] * HPG, axis=1)
    qn = jnp.concatenate([qn_ref[:, hs] for hs in heads], axis=0)
    qrs_ref[...] = jnp.concatenate([qr_ref[:, hs] for hs in heads], axis=0)

    gt_ref[...] = jax.nn.sigmoid(gz_ref[...]).T

    def gate(branch):
        return jnp.concatenate([gt_ref[pl.ds(g * (HPG * 3) + h * 3 + branch, 1), :] for h in range(HPG)], axis=1)

    cmp_end = lax.broadcasted_iota(i32, (NC, tq), 0) * CMP_STRIDE + (CMP_LEN - 1)
    cbias = jnp.where(cmp_end <= trow, 0.0, NEG_INF)
    live = (trow >= CMP_LEN - 1).astype(f32)
    sc = lax.dot_general(kc_ref[...].astype(bf16), qn, NT_DIMS, preferred_element_type=f32)
    pc = _softmax0(sc + tile4(cbias)) * tile4(live)
    ocT = lax.dot_general(vc_ref[...].astype(bf16), pc.astype(bf16), TN_DIMS, preferred_element_type=f32)
    ps = pc[:, 0:tq]
    for h in range(1, HPG):
        ps = ps + pc[:, h * tq:(h + 1) * tq]

    wl = WINDOW + tq
    start = pl.multiple_of(jnp.maximum(t0 - WINDOW, 0), tq)
    dist = trow - (start + lax.broadcasted_iota(i32, (wl, tq), 0))
    wbias = jnp.where(dist >= 0, jnp.where(dist < WINDOW, 0.0, NEG_INF), NEG_INF)
    sw = lax.dot_general(kw_ref[pl.ds(start, wl), :], qrs_ref[...], NT_DIMS, preferred_element_type=f32)
    pw = _softmax0(sw + tile4(wbias)).astype(bf16)
    owT = lax.dot_general(vw_ref[pl.ds(start, wl), :], pw, TN_DIMS, preferred_element_type=f32)
    part_ref[...] = gate(0) * ocT + gate(2) * owT

    ps_hi, ps_lo = _split_bf16(ps)
    imp = (jnp.dot(ovt_ref[...], ps_hi, preferred_element_type=f32)
           + jnp.dot(ovt_ref[...], ps_lo, preferred_element_type=f32))
    j = lax.broadcasted_iota(i32, (NS, tq), 0)
    t = t0 + lax.broadcasted_iota(i32, (NS, tq), 1)
    cur = t >> (SEL_BLOCK.bit_length() - 1)
    forced = (j == 0) | (j == cur) | (j == cur - 1)
    imp = jnp.where(forced, FORCE_SCORE, imp)
    imp = jnp.where(j * SEL_BLOCK <= t, imp, NEG_INF)
    sel = jnp.zeros((NS, tq), f32)
    for _ in range(top_n):
        m = jnp.max(imp, axis=0, keepdims=True)
        first = jnp.min(jnp.where(imp == m, j, NS), axis=0, keepdims=True)
        hit = j == first
        sel = jnp.where(hit & (m > 0.5 * NEG_INF), 1.0, sel)
        imp = jnp.where(hit, LOWEST, imp)
    sel_ref[...] = sel

    r64 = lax.broadcasted_iota(i32, (SEL_BLOCK, tq), 0)
    bpc = ck // SEL_BLOCK

    nch = (t0 + tq + ck - 1) // ck
    last = S // ck - 1

    def scores(c):
        base = pl.multiple_of(jnp.minimum(c, last) * ck, ck)
        return lax.dot_general(ks_ref[pl.ds(base, ck), :], qrs_ref[...], NT_DIMS, preferred_element_type=f32)

    def attend(c, s_ref):
        base = pl.multiple_of(c * ck, ck)
        bias = []
        for r in range(bpc):
            srow = sel_ref[pl.ds(c * bpc + r, 1), :]
            lim = jnp.where(srow > 0.5, trow, -1) - (base + SEL_BLOCK * r)
            bias.append(jnp.where(r64 <= lim, 0.0, NEG_INF))
        s = s_ref[...] + tile4(jnp.concatenate(bias, axis=0))
        m = m_ref[0:1, :]
        m_new = jnp.maximum(m, jnp.max(s, axis=0, keepdims=True))
        alpha = jnp.exp(m - m_new)
        p = jnp.exp(s - m_new)
        l_ref[0:1, :] = alpha * l_ref[0:1, :] + jnp.sum(p, axis=0, keepdims=True)
        pv = lax.dot_general(vs_ref[pl.ds(base, ck), :], p.astype(bf16), TN_DIMS, preferred_element_type=f32)
        acc_ref[...] = alpha * acc_ref[...] + pv
        m_ref[0:1, :] = m_new

    m_ref[...] = jnp.full_like(m_ref, NEG_INF)
    l_ref[...] = jnp.zeros_like(l_ref)
    acc_ref[...] = jnp.zeros_like(acc_ref)
    sa_ref[...] = scores(0)

    def pair(i, _):
        sb_ref[...] = scores(2 * i + 1)
        attend(2 * i, sa_ref)
        sa_ref[...] = scores(2 * i + 2)
        attend(2 * i + 1, sb_ref)
        return 0

    lax.fori_loop(0, nch // 2, pair, 0)

    @pl.when(nch % 2 == 1)
    def _():
        attend(nch - 1, sa_ref)

    oT = part_ref[...] + gate(1) * (acc_ref[...] * (1.0 / l_ref[0:1, :]))
    for h, hs in enumerate(heads):
        o_ref[:, hs] = oT[:, h * tq:(h + 1) * tq].T.astype(bf16)


def _nsa(qn, qr, z, kc, vc, ks, vs, kw, vw, ovt, B, S):
    tq, ck = 128, 512
    assert S % ck == 0 and S >= WINDOW + tq
    nS = S // tq
    NC, NS = S // CMP_STRIDE, S // SEL_BLOCK
    gw = HPG * HEAD_DIM
    qspec = pl.BlockSpec((tq, gw), lambda b, g, i: (b * nS + i, g))
    seq = lambda n: pl.BlockSpec((None, None, n, HEAD_DIM), lambda b, g, i: (b, g, 0, 0))
    body = functools.partial(_nsa_body, S=S, tq=tq, top_n=min(SEL_TOPK, NS), ck=ck)
    return pl.pallas_call(
        body,
        grid=(B, N_KV, nS),
        in_specs=[qspec, qspec,
                  pl.BlockSpec((tq, LANES), lambda b, g, i: (b * nS + i, C_NG // LANES)),
                  seq(NC), seq(NC), seq(S), seq(S), seq(S), seq(S),
                  pl.BlockSpec((NS, NC), lambda b, g, i: (0, 0))],
        out_specs=pl.BlockSpec((tq, gw), lambda b, g, i: (b * nS + i, g)),
        out_shape=jax.ShapeDtypeStruct((B * S, D_ATTN), bf16),
        scratch_shapes=[pltpu.VMEM((NS, tq), f32), pltpu.VMEM((LANES, tq), f32),
                        pltpu.VMEM((HPG * tq, HEAD_DIM), bf16), pltpu.VMEM((HEAD_DIM, HPG * tq), f32),
                        pltpu.VMEM((SUBLANES, HPG * tq), f32), pltpu.VMEM((SUBLANES, HPG * tq), f32),
                        pltpu.VMEM((HEAD_DIM, HPG * tq), f32),
                        pltpu.VMEM((ck, HPG * tq), f32), pltpu.VMEM((ck, HPG * tq), f32)],
        compiler_params=_params("parallel", "parallel", "arbitrary"),
        name="nsa",
    )(qn, qr, z, kc, vc, ks, vs, kw, vw, ovt)


def _extract_max(x, tag, invalid):
    m = jnp.max(x, axis=0, keepdims=True)
    first = jnp.min(jnp.where(x == m, tag, invalid), axis=0, keepdims=True)
    return m, first


def _post_body(ha_ref, ob_ref, zm0_ref, zm1_ref, x_ref, wa_ref, wb_ref, wo_ref, gf_ref, wq_ref, kbd_ref,
               h_ref, n_ref, e_ref, g_ref, st_ref, va_ref, ia_ref, et_ref, gt_ref):
    tm = x_ref.shape[0]
    K = PEER_TOPK
    ya = jnp.dot(ha_ref[...], wa_ref[...], preferred_element_type=f32)
    yb = jnp.dot(ob_ref[...], wb_ref[...], preferred_element_type=f32)
    merged = jax.nn.sigmoid(zm0_ref[...]) * ya + jax.nn.sigmoid(zm1_ref[...]) * yb
    h = x_ref[...] + jnp.dot(merged.astype(bf16), wo_ref[...], preferred_element_type=f32)
    h_ref[...] = h
    n = _rms(h, gf_ref[...]).astype(bf16)
    n_ref[...] = n
    q = jnp.dot(n, wq_ref[...], preferred_element_type=f32).astype(bf16)
    st_ref[...] = lax.dot_general(kbd_ref[...], q, NT_DIMS, preferred_element_type=f32)

    ncols = tm // LANES
    rows = lax.broadcasted_iota(i32, (N_KEYS, LANES), 0)

    def top_half(hd, _):
        chains = [(2 * hd + half, c) for half in range(2) for c in range(ncols)]
        xs = [st_ref[pl.ds(pl.multiple_of(hc * N_KEYS, N_KEYS), N_KEYS), c * LANES:(c + 1) * LANES]
              for hc, c in chains]
        for k in range(K):
            for n, (hc, c) in enumerate(chains):
                m, first = _extract_max(xs[n], rows, N_KEYS)
                va_ref[c, pl.ds(hc * K + k, 1), :] = m
                ia_ref[c, pl.ds(hc * K + k, 1), :] = first
                xs[n] = jnp.where(rows == first, LOWEST, xs[n])
        return 0

    lax.fori_loop(0, PEER_HEADS, top_half, 0)

    r8 = lax.broadcasted_iota(i32, (SUBLANES, LANES), 0)

    def combine(hd, _):
        for c in range(ncols):
            combine_block(hd, c)
        return 0

    def combine_block(hd, c):
        b1 = pl.multiple_of(hd * 2 * K, 2 * K)
        v1, v2 = va_ref[c, pl.ds(b1, K), :], va_ref[c, pl.ds(b1 + K, K), :]
        i1, i2 = ia_ref[c, pl.ds(b1, K), :], ia_ref[c, pl.ds(b1 + K, K), :]
        cand = [v1[0:1] + v2]
        tag = [lax.broadcasted_iota(i32, (K, LANES), 0)]
        eid = [(i1[0:1] * N_KEYS + i2) * EXPERT_ROWS]
        for a in range(1, SUBLANES):
            ok = r8 < K // (a + 1)
            cand.append(jnp.where(ok, v1[a:a + 1] + v2[0:SUBLANES], LOWEST))
            tag.append(a * K + r8)
            eid.append((i1[a:a + 1] * N_KEYS + i2[0:SUBLANES]) * EXPERT_ROWS)
        cand.append(v1[SUBLANES:K] + v2[0:1])
        tag.append((SUBLANES + r8) * K)
        eid.append((i1[SUBLANES:K] * N_KEYS + i2[0:1]) * EXPERT_ROWS)
        cand, tag, eid = (jnp.concatenate(v, axis=0) for v in (cand, tag, eid))
        best0 = None
        den = jnp.zeros((1, LANES), f32)
        for k in range(K):
            m, first = _extract_max(cand, tag, K * K)
            hit = tag == first
            et_ref[c, pl.ds(hd * K + k, 1), :] = jnp.sum(jnp.where(hit, eid, 0), axis=0, keepdims=True)
            cand = jnp.where(hit, LOWEST, cand)
            best0 = m if best0 is None else best0
            ex = jnp.exp(m - best0)
            gt_ref[c, pl.ds(hd * K + k, 1), :] = ex
            den = den + ex
        gt_ref[c, pl.ds(hd * K, K), :] = gt_ref[c, pl.ds(hd * K, K), :] * (1.0 / den)

    lax.fori_loop(0, PEER_HEADS, combine, 0)
    for c in range(ncols):
        e_ref[c * LANES:(c + 1) * LANES, :] = et_ref[c].T
        g_ref[c * LANES:(c + 1) * LANES, :] = gt_ref[c].T


def _post(hag, ob, z, x2, wa, wb, wo, gf, wq, kbd):
    T, D = x2.shape
    tm = min(256, T)
    HK = PEER_HEADS * PEER_TOPK
    rowblk = lambda w, c=0: pl.BlockSpec((tm, w), lambda i: (i, c))
    full = lambda shape: pl.BlockSpec(shape, lambda i: (0,) * len(shape))
    return pl.pallas_call(
        _post_body,
        grid=(T // tm,),
        in_specs=[rowblk(D), rowblk(D), rowblk(D, C_M // D), rowblk(D, C_M // D + 1), rowblk(D),
                  full((D, D)), full((D, D)), full((D, D)), full((1, D)), full((D, D)),
                  full((2 * PEER_HEADS * N_KEYS, D))],
        out_specs=[rowblk(D), rowblk(D), rowblk(HK), rowblk(HK)],
        out_shape=[jax.ShapeDtypeStruct((T, D), f32), jax.ShapeDtypeStruct((T, D), bf16),
                   jax.ShapeDtypeStruct((T, HK), i32), jax.ShapeDtypeStruct((T, HK), f32)],
        scratch_shapes=[pltpu.VMEM((2 * PEER_HEADS * N_KEYS, tm), f32),
                        pltpu.VMEM((tm // LANES, 2 * PEER_HEADS * PEER_TOPK, LANES), f32),
                        pltpu.VMEM((tm // LANES, 2 * PEER_HEADS * PEER_TOPK, LANES), i32),
                        pltpu.VMEM((tm // LANES, HK, LANES), i32), pltpu.VMEM((tm // LANES, HK, LANES), f32)],
        compiler_params=_params("parallel"),
        name="post_route",
    )(hag, ob, z, z, x2, wa, wb, wo, gf, wq, kbd)


PEER_GROUP = 8
EXPERT_ROWS = SUBLANES // 2


def _diag_mask(width):
    lane = lax.broadcasted_iota(i32, (SUBLANES, width), 1)
    sub = lax.broadcasted_iota(i32, (SUBLANES, width), 0)
    return (lane & (SUBLANES - 1)) == sub


def _gather_tiles(idx_ref, tab_ref, t):
    tiles = []
    for j in range(idx_ref.shape[1]):
        r = pl.multiple_of(idx_ref[t, j], EXPERT_ROWS)
        tiles.append(pltpu.bitcast(tab_ref[pl.ds(r, EXPERT_ROWS), :], bf16))
    return jnp.concatenate(tiles, axis=0)


def _peer_u_body(idx_ref, x_ref, g_ref, tab_ref, fold_ref, act_ref, y_ref):
    tt, hk = g_ref.shape
    diag = _diag_mask(hk * SUBLANES)

    def group(gi, _):
        for k in range(PEER_GROUP):
            t = gi * PEER_GROUP + k
            y = lax.dot_general(x_ref[t], _gather_tiles(idx_ref, tab_ref, t), NT_DIMS, preferred_element_type=f32)
            y_ref[pl.ds(t, 1), :] = jnp.sum(jnp.where(diag, y, 0.0), axis=0, keepdims=True)
        return 0

    lax.fori_loop(0, tt // PEER_GROUP, group, 0)
    y_hi, y_lo = _split_bf16(y_ref[...])
    s = (jnp.dot(y_hi, fold_ref[...], preferred_element_type=f32)
         + jnp.dot(y_lo, fold_ref[...], preferred_element_type=f32))
    act_ref[...] = jax.nn.gelu(s) * g_ref[...]


def _peer_v_body(idx_ref, act_ref, h_ref, tab_ref, spread_ref, o_ref, a_ref):
    tt, hk = act_ref.shape
    diag = _diag_mask(hk * SUBLANES)
    a_hi, a_lo = _split_bf16(act_ref[...])
    a_ref[...] = (jnp.dot(a_hi, spread_ref[...], preferred_element_type=f32)
                  + jnp.dot(a_lo, spread_ref[...], preferred_element_type=f32))

    def group(gi, _):
        for k in range(PEER_GROUP):
            t = gi * PEER_GROUP + k
            a8 = jnp.where(diag, jnp.broadcast_to(a_ref[pl.ds(t, 1), :], diag.shape), 0.0)
            lhs = jnp.concatenate(_split_bf16(a8), axis=0)
            out = jnp.dot(lhs, _gather_tiles(idx_ref, tab_ref, t), preferred_element_type=f32)
            o_ref[t] = h_ref[t] + out[0:SUBLANES] + out[SUBLANES:2 * SUBLANES]
        return 0

    lax.fori_loop(0, tt // PEER_GROUP, group, 0)


def _peer(n3, e_idx, gates, h3, u_tiles, v_tiles, fold, spread):
    T = n3.shape[0]
    HK = e_idx.shape[1]
    tt = min(64, T)
    smem_idx = pl.BlockSpec((tt, HK), lambda i: (i, 0), memory_space=pltpu.SMEM)
    tok = pl.BlockSpec((tt, SUBLANES, LANES), lambda i: (i, 0, 0))
    row = pl.BlockSpec((tt, HK), lambda i: (i, 0))
    table = pl.BlockSpec(memory_space=pltpu.VMEM)
    act = pl.pallas_call(
        _peer_u_body,
        grid=(T // tt,),
        in_specs=[smem_idx, tok, row, table, pl.BlockSpec(fold.shape, lambda i: (0, 0))],
        out_specs=row,
        out_shape=jax.ShapeDtypeStruct((T, HK), f32),
        scratch_shapes=[pltpu.VMEM((tt, HK * SUBLANES), f32)],
        compiler_params=_params("parallel"),
        name="peer_u",
    )(e_idx, n3, gates, u_tiles, fold)
    return pl.pallas_call(
        _peer_v_body,
        grid=(T // tt,),
        in_specs=[smem_idx, row, tok, table, pl.BlockSpec(spread.shape, lambda i: (0, 0))],
        out_specs=tok,
        out_shape=jax.ShapeDtypeStruct(h3.shape, f32),
        scratch_shapes=[pltpu.VMEM((tt, HK * SUBLANES), f32)],
        compiler_params=_params("parallel"),
        name="peer_v",
    )(e_idx, act, h3, v_tiles, spread)


def _ple_body(h_ref, p_ref, gp_ref, wg_ref, wp_ref, gn_ref, o_ref, *, last):
    h = h_ref[...]
    gate = jax.nn.sigmoid(jnp.dot(_rms(h, gp_ref[...]).astype(bf16), wg_ref[...], preferred_element_type=f32))
    h = h + gate * jnp.dot(p_ref[...].astype(bf16), wp_ref[...], preferred_element_type=f32)
    o_ref[...] = _rms(h, gn_ref[...]) if last else h


def _ple(h2, p2, gp, wg, wp, gn, last):
    T, D = h2.shape
    P = p2.shape[1]
    tm = min(512, T)
    full = lambda shape: pl.BlockSpec(shape, lambda i: (0,) * len(shape))
    return pl.pallas_call(
        functools.partial(_ple_body, last=last),
        grid=(T // tm,),
        in_specs=[pl.BlockSpec((tm, D), lambda i: (i, 0)), pl.BlockSpec((tm, P), lambda i: (i, 0)),
                  full((1, D)), full((D, D)), full((P, D)), full((1, D))],
        out_specs=pl.BlockSpec((tm, D), lambda i: (i, 0)),
        out_shape=jax.ShapeDtypeStruct((T, D), f32),
        compiler_params=_params("parallel"),
        name="ple_norm",
    )(h2, p2, gp, wg, wp, gn)


def _overlap_t(S):
    NC, NS = S // CMP_STRIDE, S // SEL_BLOCK
    n = np.arange(NC)[None, :] * CMP_STRIDE
    j = np.arange(NS)[:, None] * SEL_BLOCK
    ov = (n < j + SEL_BLOCK) & (n + CMP_LEN - 1 >= j) & (np.arange(NC)[None, :] < NC - 1)
    return jnp.asarray(ov, bf16)


def _pack_expert_tiles(tab):
    t = tab.astype(bf16).reshape(tab.shape[0], EXPERT_ROWS, 2, LANES)
    words = lax.bitcast_convert_type(jnp.swapaxes(t, -1, -2), i32)
    return words.reshape(tab.shape[0] * EXPERT_ROWS, LANES)


def _fold_matrix(hk):
    return jnp.asarray(np.arange(hk * SUBLANES)[:, None] // SUBLANES == np.arange(hk)[None, :], bf16)


def kernel(x, p, positions, norm_mix, w_in, conv_w, conv_b, lru_wr, lru_br, lru_wi, lru_bi, lru_lam,
           cmp_pe_k, cmp_w1_k, cmp_b1_k, cmp_w2_k, cmp_b2_k, cmp_pe_v, cmp_w1_v, cmp_b1_v, cmp_w2_v, cmp_b2_v,
           w_a, w_b, w_out, norm_ffn, peer_wq, peer_keys, peer_u, peer_v, norm_ple, ple_wg, ple_wp,
           norm_final):
    B, S, D = x.shape
    T = B * S
    depth = norm_mix.shape[0]
    HK = PEER_HEADS * PEER_TOPK
    half = HEAD_DIM // 2
    inv = ROPE_THETA ** (-jnp.arange(half, dtype=f32) / half)
    inv2 = jnp.concatenate([inv, inv])[None, :]
    sgn = jnp.concatenate([-jnp.ones((half,), f32), jnp.ones((half,), f32)])[None, :]
    pos = positions.reshape(T, 1)
    ovt = _overlap_t(S)
    fold = _fold_matrix(HK)
    spread = fold.T
    fl = CMP_LEN * HEAD_DIM
    row = lambda v: v.reshape(1, -1)

    h = x.reshape(T, D)
    for i in range(depth):
        w = w_in[i]
        sp = np.cumsum((D_RNN, D_RNN, D_ATTN, 6 * N_KV * HEAD_DIM, 3 * N_HEADS, 2 * D))
        w_ng = jnp.pad(w[:, sp[3]:sp[4]], ((0, 0), (0, NZ - C_NG - 3 * N_HEADS)))
        w_all = jnp.concatenate([w[:, :sp[2]], w[:, sp[4]:], w[:, sp[2]:sp[3]], w_ng], axis=1).astype(bf16)
        z = _inproj(h, row(norm_mix[i]), w_all)

        wri = jnp.concatenate([lru_wr[i], lru_wi[i]], axis=-1).astype(bf16)
        hag = _lru(z, B, S, conv_w[i], row(conv_b[i]), wri, row(lru_br[i]), row(lru_bi[i]), row(lru_lam[i]))

        qn, qr, kc_in, vc_in, ks, vs, kw, vw = _prep(z, pos, inv2, sgn, B, S)
        flat = lambda a: a.reshape(B, N_KV, S // CMP_STRIDE, CMP_STRIDE * HEAD_DIM)
        cmp_params = lambda pe, w1, b1, w2, b2: (
            w1.astype(bf16), jnp.broadcast_to(pe.reshape(1, fl), (SUBLANES, fl)).astype(bf16), row(b1),
            w2.astype(bf16), row(b2))
        kc, vc = _compress(flat(kc_in), flat(vc_in),
                           cmp_params(cmp_pe_k[i], cmp_w1_k[i], cmp_b1_k[i], cmp_w2_k[i], cmp_b2_k[i]),
                           cmp_params(cmp_pe_v[i], cmp_w1_v[i], cmp_b1_v[i], cmp_w2_v[i], cmp_b2_v[i]))
        ob = _nsa(qn, qr, z, kc, vc, ks, vs, kw, vw, ovt, B, S)

        keys = peer_keys[i]
        kbd = jnp.zeros((PEER_HEADS, 2, N_KEYS, PEER_HEADS, 2, PEER_HALF), f32)
        for hd in range(PEER_HEADS):
            for c in range(2):
                kbd = kbd.at[hd, c, :, hd, c, :].set(keys[c])
        kbd = kbd.reshape(2 * PEER_HEADS * N_KEYS, PEER_HEADS * PEER_QDIM).astype(bf16)
        h1, n2, e_idx, gates = _post(hag, ob, z, h, w_a[i].astype(bf16), w_b[i].astype(bf16),
                                     w_out[i].astype(bf16), row(norm_ffn[i]), peer_wq[i].astype(bf16), kbd)

        tiles = _pack_expert_tiles
        h2 = _peer(n2.reshape(T, SUBLANES, LANES), e_idx, gates, h1.reshape(T, SUBLANES, LANES),
                   tiles(peer_u[i]), tiles(peer_v[i]), fold, spread)

        h = _ple(h2.reshape(T, D), p[i].reshape(T, -1), row(norm_ple[i]), ple_wg[i].astype(bf16),
                 ple_wp[i].astype(bf16), row(norm_final), last=(i == depth - 1))
    return h.reshape(B, S, D)
```

```python
import functools

import numpy as np
import jax
import jax.numpy as jnp
from jax import lax
from jax.experimental import pallas as pl
from jax.experimental.pallas import tpu as pltpu

f32 = jnp.float32
bf16 = jnp.bfloat16
i32 = jnp.int32

EPS = 1e-6
D_RNN = 1024
RNN_BLOCKS = 8
RNN_BLOCK = D_RNN // RNN_BLOCKS
CONV_W = 4
LRU_C = 8.0
N_HEADS = 8
N_KV = 2
HPG = N_HEADS // N_KV
HEAD_DIM = 128
D_ATTN = N_HEADS * HEAD_DIM
CMP_LEN = 32
CMP_STRIDE = 16
CMP_HID = 256
SEL_BLOCK = 64
SEL_TOPK = 16
WINDOW = 512
ROPE_THETA = 10000.0
PEER_HEADS = 8
N_KEYS = 128
PEER_QDIM = 128
PEER_HALF = PEER_QDIM // 2
PEER_TOPK = 16
NEG_INF = -1e30
FORCE_SCORE = 1e3
LOWEST = -3e38
LOG2_E = 1.4426950408889634

LANES = 128
SUBLANES = 8
VMEM_LIMIT_BYTES = 56 * 1024 * 1024

C_RX, C_RG, C_Q, C_M, C_KV, C_NG = 0, 1024, 2048, 3072, 5120, 6656
NZ = 6912

NT_DIMS = (((1,), (1,)), ((), ()))
TN_DIMS = (((0,), (0,)), ((), ()))


def _params(*sem):
    return pltpu.CompilerParams(dimension_semantics=sem, vmem_limit_bytes=VMEM_LIMIT_BYTES)


def _rms(x, g):
    return x * lax.rsqrt(jnp.mean(x * x, axis=-1, keepdims=True) + EPS) * g


def _split_bf16(x):
    hi = x.astype(bf16)
    return hi, (x - hi.astype(f32)).astype(bf16)


def _inproj_body(x_ref, g_ref, w_ref, o_ref, n_ref):
    @pl.when(pl.program_id(1) == 0)
    def _():
        n_ref[...] = _rms(x_ref[...], g_ref[...]).astype(bf16)

    o_ref[...] = jnp.dot(n_ref[...], w_ref[...], preferred_element_type=f32)


def _inproj(x2, g, w):
    T, D = x2.shape
    tm, tn = min(1024, T), 768
    return pl.pallas_call(
        _inproj_body,
        grid=(T // tm, NZ // tn),
        in_specs=[pl.BlockSpec((tm, D), lambda i, j: (i, 0)),
                  pl.BlockSpec((1, D), lambda i, j: (0, 0)),
                  pl.BlockSpec((D, tn), lambda i, j: (0, j))],
        out_specs=pl.BlockSpec((tm, tn), lambda i, j: (i, j)),
        out_shape=jax.ShapeDtypeStruct((T, NZ), f32),
        scratch_shapes=[pltpu.VMEM((tm, D), bf16)],
        compiler_params=_params("parallel", "arbitrary"),
        name="inproj",
    )(x2, g, w)


def _lru_body(zx_ref, zg_ref, cw_ref, cb_ref, wri_ref, br_ref, bi_ref, lam_ref, o_ref,
              xs_ref, a_ref, u_ref, h_ref):
    ts = zx_ref.shape[0]

    @pl.when(pl.program_id(1) == 0)
    def _():
        xs_ref[0:SUBLANES, :] = jnp.zeros((SUBLANES, D_RNN), f32)
        h_ref[...] = jnp.zeros_like(h_ref)

    xs_ref[SUBLANES:, :] = zx_ref[...]
    xa = cb_ref[...]
    for k in range(CONV_W):
        off = SUBLANES - (CONV_W - 1) + k
        xa = xa + cw_ref[k:k + 1, :] * xs_ref[off:off + ts, :]
    xs_ref[0:SUBLANES, :] = xs_ref[ts:ts + SUBLANES, :]

    sp = jax.nn.softplus(-lam_ref[...])
    xb = xa.astype(bf16)
    for n in range(RNN_BLOCKS):
        sl = slice(n * RNN_BLOCK, (n + 1) * RNN_BLOCK)
        gates = jnp.dot(xb[:, sl], wri_ref[n], preferred_element_type=f32)
        r = jax.nn.sigmoid(gates[:, :RNN_BLOCK] + br_ref[:, sl])
        ig = jax.nn.sigmoid(gates[:, RNN_BLOCK:] + bi_ref[:, sl])
        log_a = -LRU_C * r * sp[:, sl]
        a = jnp.exp(log_a)
        u = jnp.sqrt(-jnp.tanh(log_a) * (a * a + 1.0)) * (ig * xa[:, sl])
        a_ref[:, sl] = a
        u_ref[:, sl] = u

    def step8(j, h):
        base = pl.multiple_of(j * SUBLANES, SUBLANES)
        for k in range(SUBLANES):
            h = a_ref[pl.ds(base + k, 1), :] * h + u_ref[pl.ds(base + k, 1), :]
            u_ref[pl.ds(base + k, 1), :] = h
        return h

    h_ref[0:1, :] = lax.fori_loop(0, ts // SUBLANES, step8, h_ref[0:1, :])
    o_ref[...] = (u_ref[...] * jax.nn.gelu(zg_ref[...])).astype(bf16)


def _lru(z, B, S, cw, cb, wri, br, bi, lam):
    ts = min(512, S)
    nS = S // ts
    vec = lambda: pl.BlockSpec((1, D_RNN), lambda b, i: (0, 0))
    return pl.pallas_call(
        _lru_body,
        grid=(B, nS),
        in_specs=[pl.BlockSpec((ts, D_RNN), lambda b, i: (b * nS + i, C_RX // D_RNN)),
                  pl.BlockSpec((ts, D_RNN), lambda b, i: (b * nS + i, C_RG // D_RNN)),
                  pl.BlockSpec((CONV_W, D_RNN), lambda b, i: (0, 0)), vec(),
                  pl.BlockSpec((RNN_BLOCKS, RNN_BLOCK, 2 * RNN_BLOCK), lambda b, i: (0, 0, 0)),
                  vec(), vec(), vec()],
        out_specs=pl.BlockSpec((ts, D_RNN), lambda b, i: (b * nS + i, 0)),
        out_shape=jax.ShapeDtypeStruct((B * S, D_RNN), bf16),
        scratch_shapes=[pltpu.VMEM((ts + SUBLANES, D_RNN), f32), pltpu.VMEM((ts, D_RNN), f32),
                        pltpu.VMEM((ts, D_RNN), f32), pltpu.VMEM((SUBLANES, D_RNN), f32)],
        compiler_params=_params("parallel", "arbitrary"),
        name="rglru",
    )(z, z, cw, cb, wri, br, bi, lam)


def _prep_body(pos_ref, inv_ref, sgn_ref, zq_ref, zc_ref, zs_ref, zw_ref,
               qn_ref, qr_ref, kc_ref, vc_ref, ks_ref, vs_ref, kw_ref, vw_ref):
    ang = pos_ref[...].astype(f32) * inv_ref[...]
    cs = jnp.cos(ang)
    sn = jnp.sin(ang) * sgn_ref[...]

    def rope(v):
        return v * cs + pltpu.roll(v, HEAD_DIM // 2, axis=1) * sn

    scale = HEAD_DIM ** -0.5 * LOG2_E
    for h in range(N_HEADS):
        sl = slice(h * HEAD_DIM, (h + 1) * HEAD_DIM)
        q = zq_ref[:, sl]
        qn_ref[:, sl] = (q * scale).astype(bf16)
        qr_ref[:, sl] = (rope(q) * scale).astype(bf16)
    for g in range(N_KV):
        ksl = slice(g * HEAD_DIM, (g + 1) * HEAD_DIM)
        vsl = slice((N_KV + g) * HEAD_DIM, (N_KV + g + 1) * HEAD_DIM)
        kc_ref[g] = zc_ref[:, ksl].astype(bf16)
        vc_ref[g] = zc_ref[:, vsl].astype(bf16)
        ks_ref[g] = rope(zs_ref[:, ksl]).astype(bf16)
        vs_ref[g] = zs_ref[:, vsl].astype(bf16)
        kw_ref[g] = rope(zw_ref[:, ksl]).astype(bf16)
        vw_ref[g] = zw_ref[:, vsl].astype(bf16)


def _prep(z, pos, inv2, sgn, B, S):
    T = B * S
    tp = min(256, S)
    nS = S // tp
    kvw = 2 * N_KV * HEAD_DIM
    row = lambda b, i: b * nS + i
    kv_out = pl.BlockSpec((None, N_KV, tp, HEAD_DIM), lambda b, i: (b, 0, i, 0))
    kv_shape = jax.ShapeDtypeStruct((B, N_KV, S, HEAD_DIM), bf16)
    q_shape = jax.ShapeDtypeStruct((T, D_ATTN), bf16)
    return pl.pallas_call(
        _prep_body,
        grid=(B, nS),
        in_specs=[pl.BlockSpec((tp, 1), lambda b, i: (row(b, i), 0)),
                  pl.BlockSpec((1, HEAD_DIM), lambda b, i: (0, 0)),
                  pl.BlockSpec((1, HEAD_DIM), lambda b, i: (0, 0)),
                  pl.BlockSpec((tp, D_ATTN), lambda b, i: (row(b, i), C_Q // D_ATTN)),
                  pl.BlockSpec((tp, kvw), lambda b, i: (row(b, i), C_KV // kvw)),
                  pl.BlockSpec((tp, kvw), lambda b, i: (row(b, i), C_KV // kvw + 1)),
                  pl.BlockSpec((tp, kvw), lambda b, i: (row(b, i), C_KV // kvw + 2))],
        out_specs=[pl.BlockSpec((tp, D_ATTN), lambda b, i: (row(b, i), 0)),
                   pl.BlockSpec((tp, D_ATTN), lambda b, i: (row(b, i), 0))] + [kv_out] * 6,
        out_shape=[q_shape, q_shape] + [kv_shape] * 6,
        compiler_params=_params("parallel", "parallel"),
        name="attn_prep",
    )(pos, inv2, sgn, z, z, z, z)


def _cmp_body(kf_ref, vf_ref, w1k_ref, pek_ref, b1k_ref, w2k_ref, b2k_ref,
              w1v_ref, pev_ref, b1v_ref, w2v_ref, b2v_ref, kc_ref, vc_ref):
    nc = kf_ref.shape[0]
    half = CMP_STRIDE * HEAD_DIM

    def mlp(f_ref, w1_ref, pe_ref, b1_ref, w2_ref, b2_ref):
        f = f_ref[...]
        first = jnp.dot(f, w1_ref[0:half, :], preferred_element_type=f32)
        second = jnp.dot(f, w1_ref[half:2 * half, :], preferred_element_type=f32)
        pew = jnp.dot(pe_ref[...], w1_ref[...], preferred_element_type=f32)[0:1]
        hid = jax.nn.gelu(first + pltpu.roll(second, nc - 1, axis=0) + pew + b1_ref[...])
        return jnp.dot(hid.astype(bf16), w2_ref[...], preferred_element_type=f32) + b2_ref[...]

    kc_ref[...] = mlp(kf_ref, w1k_ref, pek_ref, b1k_ref, w2k_ref, b2k_ref)
    vc_ref[...] = mlp(vf_ref, w1v_ref, pev_ref, b1v_ref, w2v_ref, b2v_ref)


def _compress(kflat, vflat, pk, pv):
    B, G, NC, F = kflat.shape
    fl = CMP_LEN * HEAD_DIM
    flat = pl.BlockSpec((None, None, NC, F), lambda b, g: (b, g, 0, 0))
    full = lambda shape: pl.BlockSpec(shape, lambda b, g: (0,) * len(shape))
    wspecs = [full((fl, CMP_HID)), full((SUBLANES, fl)), full((1, CMP_HID)),
              full((CMP_HID, HEAD_DIM)), full((1, HEAD_DIM))]
    out = pl.BlockSpec((None, None, NC, HEAD_DIM), lambda b, g: (b, g, 0, 0))
    shape = jax.ShapeDtypeStruct((B, G, NC, HEAD_DIM), f32)
    return pl.pallas_call(
        _cmp_body,
        grid=(B, G),
        in_specs=[flat, flat] + wspecs + wspecs,
        out_specs=[out, out],
        out_shape=[shape, shape],
        compiler_params=_params("parallel", "parallel"),
        name="compress",
    )(kflat, vflat, *pk, *pv)


def _softmax0(s):
    e = jnp.exp2(s - jnp.max(s, axis=0, keepdims=True))
    return e * (1.0 / jnp.sum(e, axis=0, keepdims=True))


def _nsa_body(qn_ref, qr_ref, gz_ref, kc_ref, vc_ref, ks_ref, vs_ref, kw_ref, vw_ref, ovt_ref,
              o_ref, sel_ref, gt_ref, qrs_ref, part_ref, m_ref, l_ref, acc_ref, sa_ref, sb_ref,
              *, S, tq, top_n, ck):
    g = pl.program_id(1)
    t0 = pl.program_id(2) * tq
    NC, NS = S // CMP_STRIDE, S // SEL_BLOCK
    heads = [slice(h * HEAD_DIM, (h + 1) * HEAD_DIM) for h in range(HPG)]
    trow = t0 + lax.broadcasted_iota(i32, (1, tq), 1)

    tile4 = lambda a: jnp.concatenate([a] * HPG, axis=1)
    qn = jnp.concatenate([qn_ref[:, hs] for hs in heads], axis=0)
    qrs_ref[...] = jnp.concatenate([qr_ref[:, hs] for hs in heads], axis=0)

    gt_ref[...] = jax.nn.sigmoid(gz_ref[...]).T

    def gate(branch):
        return jnp.concatenate([gt_ref[pl.ds(g * (HPG * 3) + h * 3 + branch, 1), :] for h in range(HPG)], axis=1)

    cmp_end = lax.broadcasted_iota(i32, (NC, tq), 0) * CMP_STRIDE + (CMP_LEN - 1)
    cbias = jnp.where(cmp_end <= trow, 0.0, NEG_INF)
    live = (trow >= CMP_LEN - 1).astype(f32)
    sc = lax.dot_general(kc_ref[...].astype(bf16), qn, NT_DIMS, preferred_element_type=f32)
    pc = _softmax0(sc + tile4(cbias)) * tile4(live)
    ocT = lax.dot_general(vc_ref[...].astype(bf16), pc.astype(bf16), TN_DIMS, preferred_element_type=f32)
    ps = pc[:, 0:tq]
    for h in range(1, HPG):
        ps = ps + pc[:, h * tq:(h + 1) * tq]

    wl = WINDOW + tq
    start = pl.multiple_of(jnp.maximum(t0 - WINDOW, 0), tq)
    dist = trow - (start + lax.broadcasted_iota(i32, (wl, tq), 0))
    wbias = jnp.where(dist >= 0, jnp.where(dist < WINDOW, 0.0, NEG_INF), NEG_INF)
    sw = lax.dot_general(kw_ref[pl.ds(start, wl), :], qrs_ref[...], NT_DIMS, preferred_element_type=f32)
    pw = _softmax0(sw + tile4(wbias)).astype(bf16)
    owT = lax.dot_general(vw_ref[pl.ds(start, wl), :], pw, TN_DIMS, preferred_element_type=f32)
    part_ref[...] = gate(0) * ocT + gate(2) * owT

    ps_hi, ps_lo = _split_bf16(ps)
    imp = (jnp.dot(ovt_ref[...], ps_hi, preferred_element_type=f32)
           + jnp.dot(ovt_ref[...], ps_lo, preferred_element_type=f32))
    j = lax.broadcasted_iota(i32, (NS, tq), 0)
    t = t0 + lax.broadcasted_iota(i32, (NS, tq), 1)
    cur = t >> (SEL_BLOCK.bit_length() - 1)
    forced = (j == 0) | (j == cur) | (j == cur - 1)
    imp = jnp.where(forced, FORCE_SCORE, imp)
    imp = jnp.where(j * SEL_BLOCK <= t, imp, NEG_INF)
    sel = jnp.zeros((NS, tq), f32)
    for _ in range(top_n):
        m = jnp.max(imp, axis=0, keepdims=True)
        first = jnp.min(jnp.where(imp == m, j, NS), axis=0, keepdims=True)
        hit = j == first
        sel = jnp.where(hit & (m > 0.5 * NEG_INF), 1.0, sel)
        imp = jnp.where(hit, LOWEST, imp)
    sel_ref[...] = sel

    r64 = lax.broadcasted_iota(i32, (SEL_BLOCK, tq), 0)
    bpc = ck // SEL_BLOCK

    nch = (t0 + tq + ck - 1) // ck
    last = S // ck - 1

    def scores(c):
        base = pl.multiple_of(jnp.minimum(c, last) * ck, ck)
        return lax.dot_general(ks_ref[pl.ds(base, ck), :], qrs_ref[...], NT_DIMS, preferred_element_type=f32)

    def attend(c, s_ref):
        base = pl.multiple_of(c * ck, ck)
        bias = []
        for r in range(bpc):
            srow = sel_ref[pl.ds(c * bpc + r, 1), :]
            lim = jnp.where(srow > 0.5, trow, -1) - (base + SEL_BLOCK * r)
            bias.append(jnp.where(r64 <= lim, 0.0, NEG_INF))
        s = s_ref[...] + tile4(jnp.concatenate(bias, axis=0))
        m = m_ref[0:1, :]
        m_new = jnp.maximum(m, jnp.max(s, axis=0, keepdims=True))
        alpha = jnp.exp2(m - m_new)
        p = jnp.exp2(s - m_new)
        l_ref[0:1, :] = alpha * l_ref[0:1, :] + jnp.sum(p, axis=0, keepdims=True)
        pv = lax.dot_general(vs_ref[pl.ds(base, ck), :], p.astype(bf16), TN_DIMS, preferred_element_type=f32)
        acc_ref[...] = alpha * acc_ref[...] + pv
        m_ref[0:1, :] = m_new

    m_ref[...] = jnp.full_like(m_ref, NEG_INF)
    l_ref[...] = jnp.zeros_like(l_ref)
    acc_ref[...] = jnp.zeros_like(acc_ref)
    sa_ref[...] = scores(0)

    def pair(i, _):
        sb_ref[...] = scores(2 * i + 1)
        attend(2 * i, sa_ref)
        sa_ref[...] = scores(2 * i + 2)
        attend(2 * i + 1, sb_ref)
        return 0

    lax.fori_loop(0, nch // 2, pair, 0)

    @pl.when(nch % 2 == 1)
    def _():
        attend(nch - 1, sa_ref)

    oT = part_ref[...] + gate(1) * (acc_ref[...] * (1.0 / l_ref[0:1, :]))
    for h, hs in enumerate(heads):
        o_ref[:, hs] = oT[:, h * tq:(h + 1) * tq].T.astype(bf16)


def _nsa(qn, qr, z, kc, vc, ks, vs, kw, vw, ovt, B, S):
    tq, ck = 128, 512
    assert S % ck == 0 and S >= WINDOW + tq
    nS = S // tq
    NC, NS = S // CMP_STRIDE, S // SEL_BLOCK
    gw = HPG * HEAD_DIM
    qspec = pl.BlockSpec((tq, gw), lambda b, g, i: (b * nS + i, g))
    seq = lambda n: pl.BlockSpec((None, None, n, HEAD_DIM), lambda b, g, i: (b, g, 0, 0))
    body = functools.partial(_nsa_body, S=S, tq=tq, top_n=min(SEL_TOPK, NS), ck=ck)
    return pl.pallas_call(
        body,
        grid=(B, N_KV, nS),
        in_specs=[qspec, qspec,
                  pl.BlockSpec((tq, LANES), lambda b, g, i: (b * nS + i, C_NG // LANES)),
                  seq(NC), seq(NC), seq(S), seq(S), seq(S), seq(S),
                  pl.BlockSpec((NS, NC), lambda b, g, i: (0, 0))],
        out_specs=pl.BlockSpec((tq, gw), lambda b, g, i: (b * nS + i, g)),
        out_shape=jax.ShapeDtypeStruct((B * S, D_ATTN), bf16),
        scratch_shapes=[pltpu.VMEM((NS, tq), f32), pltpu.VMEM((LANES, tq), f32),
                        pltpu.VMEM((HPG * tq, HEAD_DIM), bf16), pltpu.VMEM((HEAD_DIM, HPG * tq), f32),
                        pltpu.VMEM((SUBLANES, HPG * tq), f32), pltpu.VMEM((SUBLANES, HPG * tq), f32),
                        pltpu.VMEM((HEAD_DIM, HPG * tq), f32),
                        pltpu.VMEM((ck, HPG * tq), f32), pltpu.VMEM((ck, HPG * tq), f32)],
        compiler_params=_params("parallel", "parallel", "arbitrary"),
        name="nsa",
    )(qn, qr, z, kc, vc, ks, vs, kw, vw, ovt)


def _extract_max(x, tag, invalid):
    m = jnp.max(x, axis=0, keepdims=True)
    first = jnp.min(jnp.where(x == m, tag, invalid), axis=0, keepdims=True)
    return m, first


def _post_body(ha_ref, ob_ref, zm0_ref, zm1_ref, x_ref, wa_ref, wb_ref, wo_ref, gf_ref, wq_ref, kbd_ref,
               h_ref, n_ref, e_ref, g_ref, st_ref, va_ref, ia_ref, et_ref, gt_ref):
    tm = x_ref.shape[0]
    K = PEER_TOPK
    ya = jnp.dot(ha_ref[...], wa_ref[...], preferred_element_type=f32)
    yb = jnp.dot(ob_ref[...], wb_ref[...], preferred_element_type=f32)
    merged = jax.nn.sigmoid(zm0_ref[...]) * ya + jax.nn.sigmoid(zm1_ref[...]) * yb
    h = x_ref[...] + jnp.dot(merged.astype(bf16), wo_ref[...], preferred_element_type=f32)
    h_ref[...] = h
    n = _rms(h, gf_ref[...]).astype(bf16)
    n_ref[...] = n
    q = jnp.dot(n, wq_ref[...], preferred_element_type=f32).astype(bf16)
    st_ref[...] = lax.dot_general(kbd_ref[...], q, NT_DIMS, preferred_element_type=f32)

    ncols = tm // LANES
    rows = lax.broadcasted_iota(i32, (N_KEYS, LANES), 0)

    def top_half(hd, _):
        chains = [(2 * hd + half, c) for half in range(2) for c in range(ncols)]
        xs = [st_ref[pl.ds(pl.multiple_of(hc * N_KEYS, N_KEYS), N_KEYS), c * LANES:(c + 1) * LANES]
              for hc, c in chains]
        for k in range(K):
            for n, (hc, c) in enumerate(chains):
                m, first = _extract_max(xs[n], rows, N_KEYS)
                va_ref[c, pl.ds(hc * K + k, 1), :] = m
                ia_ref[c, pl.ds(hc * K + k, 1), :] = first
                xs[n] = jnp.where(rows == first, LOWEST, xs[n])
        return 0

    lax.fori_loop(0, PEER_HEADS, top_half, 0)

    r8 = lax.broadcasted_iota(i32, (SUBLANES, LANES), 0)

    def combine(hd, _):
        for c in range(ncols):
            combine_block(hd, c)
        return 0

    def combine_block(hd, c):
        b1 = pl.multiple_of(hd * 2 * K, 2 * K)
        v1, v2 = va_ref[c, pl.ds(b1, K), :], va_ref[c, pl.ds(b1 + K, K), :]
        i1, i2 = ia_ref[c, pl.ds(b1, K), :], ia_ref[c, pl.ds(b1 + K, K), :]
        cand = [v1[0:1] + v2]
        tag = [lax.broadcasted_iota(i32, (K, LANES), 0)]
        eid = [(i1[0:1] * N_KEYS + i2) * EXPERT_ROWS]
        for a in range(1, SUBLANES):
            ok = r8 < K // (a + 1)
            cand.append(jnp.where(ok, v1[a:a + 1] + v2[0:SUBLANES], LOWEST))
            tag.append(a * K + r8)
            eid.append((i1[a:a + 1] * N_KEYS + i2[0:SUBLANES]) * EXPERT_ROWS)
        cand.append(v1[SUBLANES:K] + v2[0:1])
        tag.append((SUBLANES + r8) * K)
        eid.append((i1[SUBLANES:K] * N_KEYS + i2[0:1]) * EXPERT_ROWS)
        cand, tag, eid = (jnp.concatenate(v, axis=0) for v in (cand, tag, eid))
        tag = tag * TAG_SHIFT + eid
        best0 = None
        den = jnp.zeros((1, LANES), f32)
        for k in range(K):
            m, first = _extract_max(cand, tag, K * K * TAG_SHIFT)
            hit = tag == first
            et_ref[c, pl.ds(hd * K + k, 1), :] = first & (TAG_SHIFT - 1)
            cand = jnp.where(hit, LOWEST, cand)
            best0 = m if best0 is None else best0
            ex = jnp.exp(m - best0)
            gt_ref[c, pl.ds(hd * K + k, 1), :] = ex
            den = den + ex
        gt_ref[c, pl.ds(hd * K, K), :] = gt_ref[c, pl.ds(hd * K, K), :] * (1.0 / den)

    lax.fori_loop(0, PEER_HEADS, combine, 0)
    for c in range(ncols):
        e_ref[c * LANES:(c + 1) * LANES, :] = et_ref[c].T
        g_ref[c * LANES:(c + 1) * LANES, :] = gt_ref[c].T


def _post(hag, ob, z, x2, wa, wb, wo, gf, wq, kbd):
    T, D = x2.shape
    tm = min(256, T)
    HK = PEER_HEADS * PEER_TOPK
    rowblk = lambda w, c=0: pl.BlockSpec((tm, w), lambda i: (i, c))
    full = lambda shape: pl.BlockSpec(shape, lambda i: (0,) * len(shape))
    return pl.pallas_call(
        _post_body,
        grid=(T // tm,),
        in_specs=[rowblk(D), rowblk(D), rowblk(D, C_M // D), rowblk(D, C_M // D + 1), rowblk(D),
                  full((D, D)), full((D, D)), full((D, D)), full((1, D)), full((D, D)),
                  full((2 * PEER_HEADS * N_KEYS, D))],
        out_specs=[rowblk(D), rowblk(D), rowblk(HK), rowblk(HK)],
        out_shape=[jax.ShapeDtypeStruct((T, D), f32), jax.ShapeDtypeStruct((T, D), bf16),
                   jax.ShapeDtypeStruct((T, HK), i32), jax.ShapeDtypeStruct((T, HK), f32)],
        scratch_shapes=[pltpu.VMEM((2 * PEER_HEADS * N_KEYS, tm), f32),
                        pltpu.VMEM((tm // LANES, 2 * PEER_HEADS * PEER_TOPK, LANES), f32),
                        pltpu.VMEM((tm // LANES, 2 * PEER_HEADS * PEER_TOPK, LANES), i32),
                        pltpu.VMEM((tm // LANES, HK, LANES), i32), pltpu.VMEM((tm // LANES, HK, LANES), f32)],
        compiler_params=_params("parallel"),
        name="post_route",
    )(hag, ob, z, z, x2, wa, wb, wo, gf, wq, kbd)


PEER_GROUP = 32
EXPERT_ROWS = SUBLANES // 2
TAG_SHIFT = N_KEYS * N_KEYS * EXPERT_ROWS


def _diag_mask(width):
    lane = lax.broadcasted_iota(i32, (SUBLANES, width), 1)
    sub = lax.broadcasted_iota(i32, (SUBLANES, width), 0)
    return (lane & (SUBLANES - 1)) == sub


def _gather_tiles(idx_ref, tab_ref, t):
    tiles = []
    for j in range(idx_ref.shape[1]):
        r = pl.multiple_of(idx_ref[t, j], EXPERT_ROWS)
        tiles.append(pltpu.bitcast(tab_ref[pl.ds(r, EXPERT_ROWS), :], bf16))
    return jnp.concatenate(tiles, axis=0)


def _peer_u_body(idx_ref, x_ref, g_ref, tab_ref, fold_ref, act_ref, y_ref):
    tt, hk = g_ref.shape
    diag = _diag_mask(hk * SUBLANES)

    def group(gi, _):
        for k in range(PEER_GROUP):
            t = gi * PEER_GROUP + k
            y = lax.dot_general(x_ref[t], _gather_tiles(idx_ref, tab_ref, t), NT_DIMS, preferred_element_type=f32)
            y_ref[pl.ds(t, 1), :] = jnp.sum(jnp.where(diag, y, 0.0), axis=0, keepdims=True)
        return 0

    lax.fori_loop(0, tt // PEER_GROUP, group, 0)
    y_hi, y_lo = _split_bf16(y_ref[...])
    s = (jnp.dot(y_hi, fold_ref[...], preferred_element_type=f32)
         + jnp.dot(y_lo, fold_ref[...], preferred_element_type=f32))
    act_ref[...] = jax.nn.gelu(s) * g_ref[...]


def _peer_v_body(idx_ref, act_ref, h_ref, tab_ref, spread_ref, o_ref, a_ref):
    tt, hk = act_ref.shape
    diag = _diag_mask(hk * SUBLANES)
    a_hi, a_lo = _split_bf16(act_ref[...])
    a_ref[...] = (jnp.dot(a_hi, spread_ref[...], preferred_element_type=f32)
                  + jnp.dot(a_lo, spread_ref[...], preferred_element_type=f32))

    def group(gi, _):
        for k in range(PEER_GROUP):
            t = gi * PEER_GROUP + k
            a8 = jnp.where(diag, jnp.broadcast_to(a_ref[pl.ds(t, 1), :], diag.shape), 0.0)
            lhs = jnp.concatenate(_split_bf16(a8), axis=0)
            out = jnp.dot(lhs, _gather_tiles(idx_ref, tab_ref, t), preferred_element_type=f32)
            o_ref[t] = h_ref[t] + out[0:SUBLANES] + out[SUBLANES:2 * SUBLANES]
        return 0

    lax.fori_loop(0, tt // PEER_GROUP, group, 0)


def _peer(n3, e_idx, gates, h3, u_tiles, v_tiles, fold, spread):
    T = n3.shape[0]
    HK = e_idx.shape[1]
    tt = min(128, T)
    smem_idx = pl.BlockSpec((tt, HK), lambda i: (i, 0), memory_space=pltpu.SMEM)
    tok = pl.BlockSpec((tt, SUBLANES, LANES), lambda i: (i, 0, 0))
    row = pl.BlockSpec((tt, HK), lambda i: (i, 0))
    table = pl.BlockSpec(memory_space=pltpu.VMEM)
    act = pl.pallas_call(
        _peer_u_body,
        grid=(T // tt,),
        in_specs=[smem_idx, tok, row, table, pl.BlockSpec(fold.shape, lambda i: (0, 0))],
        out_specs=row,
        out_shape=jax.ShapeDtypeStruct((T, HK), f32),
        scratch_shapes=[pltpu.VMEM((tt, HK * SUBLANES), f32)],
        compiler_params=_params("parallel"),
        name="peer_u",
    )(e_idx, n3, gates, u_tiles, fold)
    return pl.pallas_call(
        _peer_v_body,
        grid=(T // tt,),
        in_specs=[smem_idx, row, tok, table, pl.BlockSpec(spread.shape, lambda i: (0, 0))],
        out_specs=tok,
        out_shape=jax.ShapeDtypeStruct(h3.shape, f32),
        scratch_shapes=[pltpu.VMEM((tt, HK * SUBLANES), f32)],
        compiler_params=_params("parallel"),
        name="peer_v",
    )(e_idx, act, h3, v_tiles, spread)


def _ple_body(h_ref, p_ref, gp_ref, wg_ref, wp_ref, gn_ref, o_ref, *, last):
    h = h_ref[...]
    gate = jax.nn.sigmoid(jnp.dot(_rms(h, gp_ref[...]).astype(bf16), wg_ref[...], preferred_element_type=f32))
    h = h + gate * jnp.dot(p_ref[...].astype(bf16), wp_ref[...], preferred_element_type=f32)
    o_ref[...] = _rms(h, gn_ref[...]) if last else h


def _ple(h2, p2, gp, wg, wp, gn, last):
    T, D = h2.shape
    P = p2.shape[1]
    tm = min(512, T)
    full = lambda shape: pl.BlockSpec(shape, lambda i: (0,) * len(shape))
    return pl.pallas_call(
        functools.partial(_ple_body, last=last),
        grid=(T // tm,),
        in_specs=[pl.BlockSpec((tm, D), lambda i: (i, 0)), pl.BlockSpec((tm, P), lambda i: (i, 0)),
                  full((1, D)), full((D, D)), full((P, D)), full((1, D))],
        out_specs=pl.BlockSpec((tm, D), lambda i: (i, 0)),
        out_shape=jax.ShapeDtypeStruct((T, D), f32),
        compiler_params=_params("parallel"),
        name="ple_norm",
    )(h2, p2, gp, wg, wp, gn)


def _overlap_t(S):
    NC, NS = S // CMP_STRIDE, S // SEL_BLOCK
    n = np.arange(NC)[None, :] * CMP_STRIDE
    j = np.arange(NS)[:, None] * SEL_BLOCK
    ov = (n < j + SEL_BLOCK) & (n + CMP_LEN - 1 >= j) & (np.arange(NC)[None, :] < NC - 1)
    return jnp.asarray(ov, bf16)


def _pack_expert_tiles(tab):
    bits = lax.bitcast_convert_type(tab.astype(bf16), jnp.uint16).astype(jnp.uint32)
    bits = bits.reshape(tab.shape[0], EXPERT_ROWS, 2, LANES)
    words = bits[:, :, 0, :] | (bits[:, :, 1, :] << 16)
    return lax.bitcast_convert_type(words, i32).reshape(tab.shape[0] * EXPERT_ROWS, LANES)


def _fold_matrix(hk):
    return jnp.asarray(np.arange(hk * SUBLANES)[:, None] // SUBLANES == np.arange(hk)[None, :], bf16)


def kernel(x, p, positions, norm_mix, w_in, conv_w, conv_b, lru_wr, lru_br, lru_wi, lru_bi, lru_lam,
           cmp_pe_k, cmp_w1_k, cmp_b1_k, cmp_w2_k, cmp_b2_k, cmp_pe_v, cmp_w1_v, cmp_b1_v, cmp_w2_v, cmp_b2_v,
           w_a, w_b, w_out, norm_ffn, peer_wq, peer_keys, peer_u, peer_v, norm_ple, ple_wg, ple_wp,
           norm_final):
    B, S, D = x.shape
    T = B * S
    depth = norm_mix.shape[0]
    HK = PEER_HEADS * PEER_TOPK
    half = HEAD_DIM // 2
    inv = ROPE_THETA ** (-jnp.arange(half, dtype=f32) / half)
    inv2 = jnp.concatenate([inv, inv])[None, :]
    sgn = jnp.concatenate([-jnp.ones((half,), f32), jnp.ones((half,), f32)])[None, :]
    pos = positions.reshape(T, 1)
    ovt = _overlap_t(S)
    fold = _fold_matrix(HK)
    spread = fold.T
    fl = CMP_LEN * HEAD_DIM
    row = lambda v: v.reshape(1, -1)

    h = x.reshape(T, D)
    for i in range(depth):
        w = w_in[i]
        sp = np.cumsum((D_RNN, D_RNN, D_ATTN, 6 * N_KV * HEAD_DIM, 3 * N_HEADS, 2 * D))
        w_ng = jnp.pad(w[:, sp[3]:sp[4]], ((0, 0), (0, NZ - C_NG - 3 * N_HEADS)))
        w_all = jnp.concatenate([w[:, :sp[2]], w[:, sp[4]:], w[:, sp[2]:sp[3]], w_ng], axis=1).astype(bf16)
        z = _inproj(h, row(norm_mix[i]), w_all)

        wri = jnp.concatenate([lru_wr[i], lru_wi[i]], axis=-1).astype(bf16)
        hag = _lru(z, B, S, conv_w[i], row(conv_b[i]), wri, row(lru_br[i]), row(lru_bi[i]), row(lru_lam[i]))

        qn, qr, kc_in, vc_in, ks, vs, kw, vw = _prep(z, pos, inv2, sgn, B, S)
        flat = lambda a: a.reshape(B, N_KV, S // CMP_STRIDE, CMP_STRIDE * HEAD_DIM)
        cmp_params = lambda pe, w1, b1, w2, b2: (
            w1.astype(bf16), jnp.broadcast_to(pe.reshape(1, fl), (SUBLANES, fl)).astype(bf16), row(b1),
            w2.astype(bf16), row(b2))
        kc, vc = _compress(flat(kc_in), flat(vc_in),
                           cmp_params(cmp_pe_k[i], cmp_w1_k[i], cmp_b1_k[i], cmp_w2_k[i], cmp_b2_k[i]),
                           cmp_params(cmp_pe_v[i], cmp_w1_v[i], cmp_b1_v[i], cmp_w2_v[i], cmp_b2_v[i]))
        ob = _nsa(qn, qr, z, kc, vc, ks, vs, kw, vw, ovt, B, S)

        keys = peer_keys[i]
        kbd = jnp.zeros((PEER_HEADS, 2, N_KEYS, PEER_HEADS, 2, PEER_HALF), f32)
        for hd in range(PEER_HEADS):
            for c in range(2):
                kbd = kbd.at[hd, c, :, hd, c, :].set(keys[c])
        kbd = kbd.reshape(2 * PEER_HEADS * N_KEYS, PEER_HEADS * PEER_QDIM).astype(bf16)
        h1, n2, e_idx, gates = _post(hag, ob, z, h, w_a[i].astype(bf16), w_b[i].astype(bf16),
                                     w_out[i].astype(bf16), row(norm_ffn[i]), peer_wq[i].astype(bf16), kbd)

        tiles = _pack_expert_tiles
        h2 = _peer(n2.reshape(T, SUBLANES, LANES), e_idx, gates, h1.reshape(T, SUBLANES, LANES),
                   tiles(peer_u[i]), tiles(peer_v[i]), fold, spread)

        h = _ple(h2.reshape(T, D), p[i].reshape(T, -1), row(norm_ple[i]), ple_wg[i].astype(bf16),
                 ple_wp[i].astype(bf16), row(norm_final), last=(i == depth - 1))
    return h.reshape(B, S, D)
```

```python
import functools

import numpy as np
import jax
import jax.numpy as jnp
from jax import lax
from jax.experimental import pallas as pl
from jax.experimental.pallas import tpu as pltpu

f32 = jnp.float32
bf16 = jnp.bfloat16
i32 = jnp.int32

EPS = 1e-6
D_RNN = 1024
RNN_BLOCKS = 8
RNN_BLOCK = D_RNN // RNN_BLOCKS
CONV_W = 4
LRU_C = 8.0
N_HEADS = 8
N_KV = 2
HPG = N_HEADS // N_KV
HEAD_DIM = 128
D_ATTN = N_HEADS * HEAD_DIM
CMP_LEN = 32
CMP_STRIDE = 16
CMP_HID = 256
SEL_BLOCK = 64
SEL_TOPK = 16
WINDOW = 512
ROPE_THETA = 10000.0
PEER_HEADS = 8
N_KEYS = 128
PEER_QDIM = 128
PEER_HALF = PEER_QDIM // 2
PEER_TOPK = 16
NEG_INF = -1e30
FORCE_SCORE = 1e3
LOWEST = -3e38
LOG2_E = 1.4426950408889634

LANES = 128
SUBLANES = 8
VMEM_LIMIT_BYTES = 56 * 1024 * 1024

C_RX, C_RG, C_Q, C_M, C_KV, C_NG = 0, 1024, 2048, 3072, 5120, 6656
NZ = 6912

NT_DIMS = (((1,), (1,)), ((), ()))
TN_DIMS = (((0,), (0,)), ((), ()))


def _params(*sem):
    return pltpu.CompilerParams(dimension_semantics=sem, vmem_limit_bytes=VMEM_LIMIT_BYTES)


def _rms(x, g):
    return x * lax.rsqrt(jnp.mean(x * x, axis=-1, keepdims=True) + EPS) * g


def _split_bf16(x):
    hi = x.astype(bf16)
    return hi, (x - hi.astype(f32)).astype(bf16)


def _inproj_body(x_ref, g_ref, w_ref, o_ref, n_ref):
    @pl.when(pl.program_id(1) == 0)
    def _():
        n_ref[...] = _rms(x_ref[...], g_ref[...]).astype(bf16)

    o_ref[...] = jnp.dot(n_ref[...], w_ref[...], preferred_element_type=f32)


def _inproj(x2, g, w):
    T, D = x2.shape
    tm, tn = min(1024, T), 768
    return pl.pallas_call(
        _inproj_body,
        grid=(T // tm, NZ // tn),
        in_specs=[pl.BlockSpec((tm, D), lambda i, j: (i, 0)),
                  pl.BlockSpec((1, D), lambda i, j: (0, 0)),
                  pl.BlockSpec((D, tn), lambda i, j: (0, j))],
        out_specs=pl.BlockSpec((tm, tn), lambda i, j: (i, j)),
        out_shape=jax.ShapeDtypeStruct((T, NZ), f32),
        scratch_shapes=[pltpu.VMEM((tm, D), bf16)],
        compiler_params=_params("parallel", "arbitrary"),
        name="inproj",
    )(x2, g, w)


def _lru_body(zx_ref, zg_ref, cw_ref, cb_ref, wri_ref, br_ref, bi_ref, lam_ref, o_ref,
              xs_ref, a_ref, u_ref, h_ref):
    ts = zx_ref.shape[0]

    @pl.when(pl.program_id(1) == 0)
    def _():
        xs_ref[0:SUBLANES, :] = jnp.zeros((SUBLANES, D_RNN), f32)
        h_ref[...] = jnp.zeros_like(h_ref)

    xs_ref[SUBLANES:, :] = zx_ref[...]
    xa = cb_ref[...]
    for k in range(CONV_W):
        off = SUBLANES - (CONV_W - 1) + k
        xa = xa + cw_ref[k:k + 1, :] * xs_ref[off:off + ts, :]
    xs_ref[0:SUBLANES, :] = xs_ref[ts:ts + SUBLANES, :]

    sp = jax.nn.softplus(-lam_ref[...])
    xb = xa.astype(bf16)
    for n in range(RNN_BLOCKS):
        sl = slice(n * RNN_BLOCK, (n + 1) * RNN_BLOCK)
        gates = jnp.dot(xb[:, sl], wri_ref[n], preferred_element_type=f32)
        r = jax.nn.sigmoid(gates[:, :RNN_BLOCK] + br_ref[:, sl])
        ig = jax.nn.sigmoid(gates[:, RNN_BLOCK:] + bi_ref[:, sl])
        log_a = -LRU_C * r * sp[:, sl]
        a = jnp.exp(log_a)
        u = jnp.sqrt(-jnp.tanh(log_a) * (a * a + 1.0)) * (ig * xa[:, sl])
        a_ref[:, sl] = a
        u_ref[:, sl] = u

    def step8(j, h):
        base = pl.multiple_of(j * SUBLANES, SUBLANES)
        for k in range(SUBLANES):
            h = a_ref[pl.ds(base + k, 1), :] * h + u_ref[pl.ds(base + k, 1), :]
            u_ref[pl.ds(base + k, 1), :] = h
        return h

    h_ref[0:1, :] = lax.fori_loop(0, ts // SUBLANES, step8, h_ref[0:1, :])
    o_ref[...] = (u_ref[...] * jax.nn.gelu(zg_ref[...])).astype(bf16)


def _lru(z, B, S, cw, cb, wri, br, bi, lam):
    ts = min(512, S)
    nS = S // ts
    vec = lambda: pl.BlockSpec((1, D_RNN), lambda b, i: (0, 0))
    return pl.pallas_call(
        _lru_body,
        grid=(B, nS),
        in_specs=[pl.BlockSpec((ts, D_RNN), lambda b, i: (b * nS + i, C_RX // D_RNN)),
                  pl.BlockSpec((ts, D_RNN), lambda b, i: (b * nS + i, C_RG // D_RNN)),
                  pl.BlockSpec((CONV_W, D_RNN), lambda b, i: (0, 0)), vec(),
                  pl.BlockSpec((RNN_BLOCKS, RNN_BLOCK, 2 * RNN_BLOCK), lambda b, i: (0, 0, 0)),
                  vec(), vec(), vec()],
        out_specs=pl.BlockSpec((ts, D_RNN), lambda b, i: (b * nS + i, 0)),
        out_shape=jax.ShapeDtypeStruct((B * S, D_RNN), bf16),
        scratch_shapes=[pltpu.VMEM((ts + SUBLANES, D_RNN), f32), pltpu.VMEM((ts, D_RNN), f32),
                        pltpu.VMEM((ts, D_RNN), f32), pltpu.VMEM((SUBLANES, D_RNN), f32)],
        compiler_params=_params("parallel", "arbitrary"),
        name="rglru",
    )(z, z, cw, cb, wri, br, bi, lam)


def _prep_body(pos_ref, inv_ref, sgn_ref, zq_ref, zc_ref, zs_ref, zw_ref,
               qn_ref, qr_ref, kc_ref, vc_ref, ks_ref, vs_ref, kw_ref, vw_ref):
    ang = pos_ref[...].astype(f32) * inv_ref[...]
    cs = jnp.cos(ang)
    sn = jnp.sin(ang) * sgn_ref[...]

    def rope(v):
        return v * cs + pltpu.roll(v, HEAD_DIM // 2, axis=1) * sn

    scale = HEAD_DIM ** -0.5 * LOG2_E
    for h in range(N_HEADS):
        sl = slice(h * HEAD_DIM, (h + 1) * HEAD_DIM)
        q = zq_ref[:, sl]
        qn_ref[:, sl] = (q * scale).astype(bf16)
        qr_ref[:, sl] = (rope(q) * scale).astype(bf16)
    for g in range(N_KV):
        ksl = slice(g * HEAD_DIM, (g + 1) * HEAD_DIM)
        vsl = slice((N_KV + g) * HEAD_DIM, (N_KV + g + 1) * HEAD_DIM)
        kc_ref[g] = zc_ref[:, ksl].astype(bf16)
        vc_ref[g] = zc_ref[:, vsl].astype(bf16)
        ks_ref[g] = rope(zs_ref[:, ksl]).astype(bf16)
        vs_ref[g] = zs_ref[:, vsl].astype(bf16)
        kw_ref[g] = rope(zw_ref[:, ksl]).astype(bf16)
        vw_ref[g] = zw_ref[:, vsl].astype(bf16)


def _prep(z, pos, inv2, sgn, B, S):
    T = B * S
    tp = min(256, S)
    nS = S // tp
    kvw = 2 * N_KV * HEAD_DIM
    row = lambda b, i: b * nS + i
    kv_out = pl.BlockSpec((None, N_KV, tp, HEAD_DIM), lambda b, i: (b, 0, i, 0))
    kv_shape = jax.ShapeDtypeStruct((B, N_KV, S, HEAD_DIM), bf16)
    q_shape = jax.ShapeDtypeStruct((T, D_ATTN), bf16)
    return pl.pallas_call(
        _prep_body,
        grid=(B, nS),
        in_specs=[pl.BlockSpec((tp, 1), lambda b, i: (row(b, i), 0)),
                  pl.BlockSpec((1, HEAD_DIM), lambda b, i: (0, 0)),
                  pl.BlockSpec((1, HEAD_DIM), lambda b, i: (0, 0)),
                  pl.BlockSpec((tp, D_ATTN), lambda b, i: (row(b, i), C_Q // D_ATTN)),
                  pl.BlockSpec((tp, kvw), lambda b, i: (row(b, i), C_KV // kvw)),
                  pl.BlockSpec((tp, kvw), lambda b, i: (row(b, i), C_KV // kvw + 1)),
                  pl.BlockSpec((tp, kvw), lambda b, i: (row(b, i), C_KV // kvw + 2))],
        out_specs=[pl.BlockSpec((tp, D_ATTN), lambda b, i: (row(b, i), 0)),
                   pl.BlockSpec((tp, D_ATTN), lambda b, i: (row(b, i), 0))] + [kv_out] * 6,
        out_shape=[q_shape, q_shape] + [kv_shape] * 6,
        compiler_params=_params("parallel", "parallel"),
        name="attn_prep",
    )(pos, inv2, sgn, z, z, z, z)


def _cmp_body(kf_ref, vf_ref, w1k_ref, pek_ref, b1k_ref, w2k_ref, b2k_ref,
              w1v_ref, pev_ref, b1v_ref, w2v_ref, b2v_ref, kc_ref, vc_ref):
    nc = kf_ref.shape[0]
    half = CMP_STRIDE * HEAD_DIM

    def mlp(f_ref, w1_ref, pe_ref, b1_ref, w2_ref, b2_ref):
        f = f_ref[...]
        first = jnp.dot(f, w1_ref[0:half, :], preferred_element_type=f32)
        second = jnp.dot(f, w1_ref[half:2 * half, :], preferred_element_type=f32)
        pew = jnp.dot(pe_ref[...], w1_ref[...], preferred_element_type=f32)[0:1]
        hid = jax.nn.gelu(first + pltpu.roll(second, nc - 1, axis=0) + pew + b1_ref[...])
        return jnp.dot(hid.astype(bf16), w2_ref[...], preferred_element_type=f32) + b2_ref[...]

    kc_ref[...] = mlp(kf_ref, w1k_ref, pek_ref, b1k_ref, w2k_ref, b2k_ref)
    vc_ref[...] = mlp(vf_ref, w1v_ref, pev_ref, b1v_ref, w2v_ref, b2v_ref)


def _compress(kflat, vflat, pk, pv):
    B, G, NC, F = kflat.shape
    fl = CMP_LEN * HEAD_DIM
    flat = pl.BlockSpec((None, None, NC, F), lambda b, g: (b, g, 0, 0))
    full = lambda shape: pl.BlockSpec(shape, lambda b, g: (0,) * len(shape))
    wspecs = [full((fl, CMP_HID)), full((SUBLANES, fl)), full((1, CMP_HID)),
              full((CMP_HID, HEAD_DIM)), full((1, HEAD_DIM))]
    out = pl.BlockSpec((None, None, NC, HEAD_DIM), lambda b, g: (b, g, 0, 0))
    shape = jax.ShapeDtypeStruct((B, G, NC, HEAD_DIM), f32)
    return pl.pallas_call(
        _cmp_body,
        grid=(B, G),
        in_specs=[flat, flat] + wspecs + wspecs,
        out_specs=[out, out],
        out_shape=[shape, shape],
        compiler_params=_params("parallel", "parallel"),
        name="compress",
    )(kflat, vflat, *pk, *pv)


def _softmax0(s):
    e = jnp.exp2(s - jnp.max(s, axis=0, keepdims=True))
    return e * (1.0 / jnp.sum(e, axis=0, keepdims=True))


def _nsa_body(qn_ref, qr_ref, gz_ref, kc_ref, vc_ref, ks_ref, vs_ref, kw_ref, vw_ref, ovt_ref,
              o_ref, sel_ref, gt_ref, qrs_ref, part_ref, m_ref, l_ref, acc_ref, sa_ref, sb_ref,
              *, S, tq, top_n, ck):
    g = pl.program_id(1)
    t0 = pl.program_id(2) * tq
    NC, NS = S // CMP_STRIDE, S // SEL_BLOCK
    heads = [slice(h * HEAD_DIM, (h + 1) * HEAD_DIM) for h in range(HPG)]
    trow = t0 + lax.broadcasted_iota(i32, (1, tq), 1)

    tile4 = lambda a: jnp.concatenate([a] * HPG, axis=1)
    qn = jnp.concatenate([qn_ref[:, hs] for hs in heads], axis=0)
    qrs_ref[...] = jnp.concatenate([qr_ref[:, hs] for hs in heads], axis=0)

    gt_ref[...] = jax.nn.sigmoid(gz_ref[...]).T

    def gate(branch):
        return jnp.concatenate([gt_ref[pl.ds(g * (HPG * 3) + h * 3 + branch, 1), :] for h in range(HPG)], axis=1)

    cmp_end = lax.broadcasted_iota(i32, (NC, tq), 0) * CMP_STRIDE + (CMP_LEN - 1)
    cbias = jnp.where(cmp_end <= trow, 0.0, NEG_INF)
    live = (trow >= CMP_LEN - 1).astype(f32)
    sc = lax.dot_general(kc_ref[...].astype(bf16), qn, NT_DIMS, preferred_element_type=f32)
    pc = _softmax0(sc + tile4(cbias)) * tile4(live)
    ocT = lax.dot_general(vc_ref[...].astype(bf16), pc.astype(bf16), TN_DIMS, preferred_element_type=f32)
    ps = pc[:, 0:tq]
    for h in range(1, HPG):
        ps = ps + pc[:, h * tq:(h + 1) * tq]

    wl = WINDOW + tq
    start = pl.multiple_of(jnp.maximum(t0 - WINDOW, 0), tq)
    dist = trow - (start + lax.broadcasted_iota(i32, (wl, tq), 0))
    wbias = jnp.where(dist >= 0, jnp.where(dist < WINDOW, 0.0, NEG_INF), NEG_INF)
    sw = lax.dot_general(kw_ref[pl.ds(start, wl), :], qrs_ref[...], NT_DIMS, preferred_element_type=f32)
    pw = _softmax0(sw + tile4(wbias)).astype(bf16)
    owT = lax.dot_general(vw_ref[pl.ds(start, wl), :], pw, TN_DIMS, preferred_element_type=f32)
    part_ref[...] = gate(0) * ocT + gate(2) * owT

    ps_hi, ps_lo = _split_bf16(ps)
    imp = (jnp.dot(ovt_ref[...], ps_hi, preferred_element_type=f32)
           + jnp.dot(ovt_ref[...], ps_lo, preferred_element_type=f32))
    j = lax.broadcasted_iota(i32, (NS, tq), 0)
    t = t0 + lax.broadcasted_iota(i32, (NS, tq), 1)
    cur = t >> (SEL_BLOCK.bit_length() - 1)
    forced = (j == 0) | (j == cur) | (j == cur - 1)
    imp = jnp.where(forced, FORCE_SCORE, imp)
    imp = jnp.where(j * SEL_BLOCK <= t, imp, NEG_INF)
    sel = jnp.zeros((NS, tq), f32)
    for _ in range(top_n):
        m = jnp.max(imp, axis=0, keepdims=True)
        first = jnp.min(jnp.where(imp == m, j, NS), axis=0, keepdims=True)
        hit = j == first
        sel = jnp.where(hit & (m > 0.5 * NEG_INF), 1.0, sel)
        imp = jnp.where(hit, LOWEST, imp)
    sel_ref[...] = sel

    r64 = lax.broadcasted_iota(i32, (SEL_BLOCK, tq), 0)
    bpc = ck // SEL_BLOCK

    nch = (t0 + tq + ck - 1) // ck
    last = S // ck - 1

    def scores(c):
        base = pl.multiple_of(jnp.minimum(c, last) * ck, ck)
        return lax.dot_general(ks_ref[pl.ds(base, ck), :], qrs_ref[...], NT_DIMS, preferred_element_type=f32)

    def attend(c, s_ref):
        base = pl.multiple_of(c * ck, ck)
        bias = []
        for r in range(bpc):
            srow = sel_ref[pl.ds(c * bpc + r, 1), :]
            lim = jnp.where(srow > 0.5, trow, -1) - (base + SEL_BLOCK * r)
            bias.append(jnp.where(r64 <= lim, 0.0, NEG_INF))
        s = s_ref[...] + tile4(jnp.concatenate(bias, axis=0))
        m = m_ref[0:1, :]
        m_new = jnp.maximum(m, jnp.max(s, axis=0, keepdims=True))
        alpha = jnp.exp2(m - m_new)
        p = jnp.exp2(s - m_new)
        l_ref[0:1, :] = alpha * l_ref[0:1, :] + jnp.sum(p, axis=0, keepdims=True)
        pv = lax.dot_general(vs_ref[pl.ds(base, ck), :], p.astype(bf16), TN_DIMS, preferred_element_type=f32)
        acc_ref[...] = alpha * acc_ref[...] + pv
        m_ref[0:1, :] = m_new

    m_ref[...] = jnp.full_like(m_ref, NEG_INF)
    l_ref[...] = jnp.zeros_like(l_ref)
    acc_ref[...] = jnp.zeros_like(acc_ref)
    sa_ref[...] = scores(0)

    def pair(i, _):
        sb_ref[...] = scores(2 * i + 1)
        attend(2 * i, sa_ref)
        sa_ref[...] = scores(2 * i + 2)
        attend(2 * i + 1, sb_ref)
        return 0

    lax.fori_loop(0, nch // 2, pair, 0)

    @pl.when(nch % 2 == 1)
    def _():
        attend(nch - 1, sa_ref)

    oT = part_ref[...] + gate(1) * (acc_ref[...] * (1.0 / l_ref[0:1, :]))
    for h, hs in enumerate(heads):
        o_ref[:, hs] = oT[:, h * tq:(h + 1) * tq].T.astype(bf16)


def _nsa(qn, qr, z, kc, vc, ks, vs, kw, vw, ovt, B, S):
    tq, ck = 128, 512
    assert S % ck == 0 and S >= WINDOW + tq
    nS = S // tq
    NC, NS = S // CMP_STRIDE, S // SEL_BLOCK
    gw = HPG * HEAD_DIM
    qspec = pl.BlockSpec((tq, gw), lambda b, g, i: (b * nS + i, g))
    seq = lambda n: pl.BlockSpec((None, None, n, HEAD_DIM), lambda b, g, i: (b, g, 0, 0))
    body = functools.partial(_nsa_body, S=S, tq=tq, top_n=min(SEL_TOPK, NS), ck=ck)
    return pl.pallas_call(
        body,
        grid=(B, N_KV, nS),
        in_specs=[qspec, qspec,
                  pl.BlockSpec((tq, LANES), lambda b, g, i: (b * nS + i, C_NG // LANES)),
                  seq(NC), seq(NC), seq(S), seq(S), seq(S), seq(S),
                  pl.BlockSpec((NS, NC), lambda b, g, i: (0, 0))],
        out_specs=pl.BlockSpec((tq, gw), lambda b, g, i: (b * nS + i, g)),
        out_shape=jax.ShapeDtypeStruct((B * S, D_ATTN), bf16),
        scratch_shapes=[pltpu.VMEM((NS, tq), f32), pltpu.VMEM((LANES, tq), f32),
                        pltpu.VMEM((HPG * tq, HEAD_DIM), bf16), pltpu.VMEM((HEAD_DIM, HPG * tq), f32),
                        pltpu.VMEM((SUBLANES, HPG * tq), f32), pltpu.VMEM((SUBLANES, HPG * tq), f32),
                        pltpu.VMEM((HEAD_DIM, HPG * tq), f32),
                        pltpu.VMEM((ck, HPG * tq), f32), pltpu.VMEM((ck, HPG * tq), f32)],
        compiler_params=_params("parallel", "parallel", "arbitrary"),
        name="nsa",
    )(qn, qr, z, kc, vc, ks, vs, kw, vw, ovt)


def _extract_max(x, tag, invalid):
    m = jnp.max(x, axis=0, keepdims=True)
    first = jnp.min(jnp.where(x == m, tag, invalid), axis=0, keepdims=True)
    return m, first


def _post_body(ha_ref, ob_ref, zm0_ref, zm1_ref, x_ref, wa_ref, wb_ref, wo_ref, gf_ref, wq_ref, kbd_ref,
               h_ref, n_ref, e_ref, g_ref, st_ref, va_ref, ia_ref, et_ref, gt_ref):
    tm = x_ref.shape[0]
    K = PEER_TOPK
    ya = jnp.dot(ha_ref[...], wa_ref[...], preferred_element_type=f32)
    yb = jnp.dot(ob_ref[...], wb_ref[...], preferred_element_type=f32)
    merged = jax.nn.sigmoid(zm0_ref[...]) * ya + jax.nn.sigmoid(zm1_ref[...]) * yb
    h = x_ref[...] + jnp.dot(merged.astype(bf16), wo_ref[...], preferred_element_type=f32)
    h_ref[...] = h
    n = _rms(h, gf_ref[...]).astype(bf16)
    n_ref[...] = n
    q = jnp.dot(n, wq_ref[...], preferred_element_type=f32).astype(bf16)
    st_ref[...] = lax.dot_general(kbd_ref[...], q, NT_DIMS, preferred_element_type=f32)

    ncols = tm // LANES
    rows = lax.broadcasted_iota(i32, (N_KEYS, LANES), 0)

    def top_half(hd, _):
        chains = [(2 * hd + half, c) for half in range(2) for c in range(ncols)]
        xs = [st_ref[pl.ds(pl.multiple_of(hc * N_KEYS, N_KEYS), N_KEYS), c * LANES:(c + 1) * LANES]
              for hc, c in chains]
        for k in range(K):
            for n, (hc, c) in enumerate(chains):
                m, first = _extract_max(xs[n], rows, N_KEYS)
                va_ref[c, pl.ds(hc * K + k, 1), :] = m
                ia_ref[c, pl.ds(hc * K + k, 1), :] = first
                xs[n] = jnp.where(rows == first, LOWEST, xs[n])
        return 0

    lax.fori_loop(0, PEER_HEADS, top_half, 0)

    r8 = lax.broadcasted_iota(i32, (SUBLANES, LANES), 0)

    def combine(hd, _):
        for c in range(ncols):
            combine_block(hd, c)
        return 0

    def combine_block(hd, c):
        b1 = pl.multiple_of(hd * 2 * K, 2 * K)
        v1, v2 = va_ref[c, pl.ds(b1, K), :], va_ref[c, pl.ds(b1 + K, K), :]
        i1, i2 = ia_ref[c, pl.ds(b1, K), :], ia_ref[c, pl.ds(b1 + K, K), :]
        cand = [v1[0:1] + v2]
        tag = [lax.broadcasted_iota(i32, (K, LANES), 0)]
        eid = [(i1[0:1] * N_KEYS + i2) * EXPERT_ROWS]
        for a in range(1, SUBLANES):
            ok = r8 < K // (a + 1)
            cand.append(jnp.where(ok, v1[a:a + 1] + v2[0:SUBLANES], LOWEST))
            tag.append(a * K + r8)
            eid.append((i1[a:a + 1] * N_KEYS + i2[0:SUBLANES]) * EXPERT_ROWS)
        cand.append(v1[SUBLANES:K] + v2[0:1])
        tag.append((SUBLANES + r8) * K)
        eid.append((i1[SUBLANES:K] * N_KEYS + i2[0:1]) * EXPERT_ROWS)
        cand, tag, eid = (jnp.concatenate(v, axis=0) for v in (cand, tag, eid))
        tag = tag * TAG_SHIFT + eid
        best0 = None
        den = jnp.zeros((1, LANES), f32)
        for k in range(K):
            m, first = _extract_max(cand, tag, K * K * TAG_SHIFT)
            hit = tag == first
            et_ref[c, pl.ds(hd * K + k, 1), :] = first & (TAG_SHIFT - 1)
            cand = jnp.where(hit, LOWEST, cand)
            best0 = m if best0 is None else best0
            ex = jnp.exp(m - best0)
            gt_ref[c, pl.ds(hd * K + k, 1), :] = ex
            den = den + ex
        gt_ref[c, pl.ds(hd * K, K), :] = gt_ref[c, pl.ds(hd * K, K), :] * (1.0 / den)

    lax.fori_loop(0, PEER_HEADS, combine, 0)
    for c in range(ncols):
        e_ref[c * LANES:(c + 1) * LANES, :] = et_ref[c].T
        g_ref[c * LANES:(c + 1) * LANES, :] = gt_ref[c].T


def _post(hag, ob, z, x2, wa, wb, wo, gf, wq, kbd):
    T, D = x2.shape
    tm = min(256, T)
    HK = PEER_HEADS * PEER_TOPK
    rowblk = lambda w, c=0: pl.BlockSpec((tm, w), lambda i: (i, c))
    full = lambda shape: pl.BlockSpec(shape, lambda i: (0,) * len(shape))
    return pl.pallas_call(
        _post_body,
        grid=(T // tm,),
        in_specs=[rowblk(D), rowblk(D), rowblk(D, C_M // D), rowblk(D, C_M // D + 1), rowblk(D),
                  full((D, D)), full((D, D)), full((D, D)), full((1, D)), full((D, D)),
                  full((2 * PEER_HEADS * N_KEYS, D))],
        out_specs=[rowblk(D), rowblk(D), rowblk(HK), rowblk(HK)],
        out_shape=[jax.ShapeDtypeStruct((T, D), f32), jax.ShapeDtypeStruct((T, D), bf16),
                   jax.ShapeDtypeStruct((T, HK), i32), jax.ShapeDtypeStruct((T, HK), f32)],
        scratch_shapes=[pltpu.VMEM((2 * PEER_HEADS * N_KEYS, tm), f32),
                        pltpu.VMEM((tm // LANES, 2 * PEER_HEADS * PEER_TOPK, LANES), f32),
                        pltpu.VMEM((tm // LANES, 2 * PEER_HEADS * PEER_TOPK, LANES), i32),
                        pltpu.VMEM((tm // LANES, HK, LANES), i32), pltpu.VMEM((tm // LANES, HK, LANES), f32)],
        compiler_params=_params("parallel"),
        name="post_route",
    )(hag, ob, z, z, x2, wa, wb, wo, gf, wq, kbd)


PEER_GROUP = 32
EXPERT_ROWS = SUBLANES // 2
TAG_SHIFT = N_KEYS * N_KEYS * EXPERT_ROWS


def _diag_mask(width):
    lane = lax.broadcasted_iota(i32, (SUBLANES, width), 1)
    sub = lax.broadcasted_iota(i32, (SUBLANES, width), 0)
    return (lane & (SUBLANES - 1)) == sub


def _gather_tiles(idx_ref, tab_ref, t):
    tiles = []
    for j in range(idx_ref.shape[1]):
        r = pl.multiple_of(idx_ref[t, j], EXPERT_ROWS)
        tiles.append(pltpu.bitcast(tab_ref[pl.ds(r, EXPERT_ROWS), :], bf16))
    return jnp.concatenate(tiles, axis=0)


def _peer_u_body(idx_ref, x_ref, g_ref, tab_ref, fold_ref, act_ref, y_ref):
    tt, hk = g_ref.shape
    diag = _diag_mask(hk * SUBLANES)

    def group(gi, _):
        for k in range(PEER_GROUP):
            t = gi * PEER_GROUP + k
            y = lax.dot_general(x_ref[t], _gather_tiles(idx_ref, tab_ref, t), NT_DIMS, preferred_element_type=f32)
            y_ref[pl.ds(t, 1), :] = jnp.sum(jnp.where(diag, y, 0.0), axis=0, keepdims=True)
        return 0

    lax.fori_loop(0, tt // PEER_GROUP, group, 0)
    y_hi, y_lo = _split_bf16(y_ref[...])
    s = (jnp.dot(y_hi, fold_ref[...], preferred_element_type=f32)
         + jnp.dot(y_lo, fold_ref[...], preferred_element_type=f32))
    act_ref[...] = jax.nn.gelu(s) * g_ref[...]


def _peer_v_body(idx_ref, act_ref, h_ref, tab_ref, spread_ref, o_ref, a_ref):
    tt, hk = act_ref.shape
    diag = _diag_mask(hk * SUBLANES)
    a_hi, a_lo = _split_bf16(act_ref[...])
    a_ref[...] = (jnp.dot(a_hi, spread_ref[...], preferred_element_type=f32)
                  + jnp.dot(a_lo, spread_ref[...], preferred_element_type=f32))

    def group(gi, _):
        for k in range(PEER_GROUP):
            t = gi * PEER_GROUP + k
            a8 = jnp.where(diag, jnp.broadcast_to(a_ref[pl.ds(t, 1), :], diag.shape), 0.0)
            lhs = jnp.concatenate(_split_bf16(a8), axis=0)
            out = jnp.dot(lhs, _gather_tiles(idx_ref, tab_ref, t), preferred_element_type=f32)
            o_ref[t] = h_ref[t] + out[0:SUBLANES] + out[SUBLANES:2 * SUBLANES]
        return 0

    lax.fori_loop(0, tt // PEER_GROUP, group, 0)


def _peer(n3, e_idx, gates, h3, u_tiles, v_tiles, fold, spread):
    T = n3.shape[0]
    HK = e_idx.shape[1]
    tt = min(256, T)
    smem_idx = pl.BlockSpec((tt, HK), lambda i: (i, 0), memory_space=pltpu.SMEM)
    tok = pl.BlockSpec((tt, SUBLANES, LANES), lambda i: (i, 0, 0))
    row = pl.BlockSpec((tt, HK), lambda i: (i, 0))
    table = pl.BlockSpec(memory_space=pltpu.VMEM)
    act = pl.pallas_call(
        _peer_u_body,
        grid=(T // tt,),
        in_specs=[smem_idx, tok, row, table, pl.BlockSpec(fold.shape, lambda i: (0, 0))],
        out_specs=row,
        out_shape=jax.ShapeDtypeStruct((T, HK), f32),
        scratch_shapes=[pltpu.VMEM((tt, HK * SUBLANES), f32)],
        compiler_params=_params("parallel"),
        name="peer_u",
    )(e_idx, n3, gates, u_tiles, fold)
    return pl.pallas_call(
        _peer_v_body,
        grid=(T // tt,),
        in_specs=[smem_idx, row, tok, table, pl.BlockSpec(spread.shape, lambda i: (0, 0))],
        out_specs=tok,
        out_shape=jax.ShapeDtypeStruct(h3.shape, f32),
        scratch_shapes=[pltpu.VMEM((tt, HK * SUBLANES), f32)],
        compiler_params=_params("parallel"),
        name="peer_v",
    )(e_idx, act, h3, v_tiles, spread)


def _ple_body(h_ref, p_ref, gp_ref, wg_ref, wp_ref, gn_ref, o_ref, *, last):
    h = h_ref[...]
    gate = jax.nn.sigmoid(jnp.dot(_rms(h, gp_ref[...]).astype(bf16), wg_ref[...], preferred_element_type=f32))
    h = h + gate * jnp.dot(p_ref[...].astype(bf16), wp_ref[...], preferred_element_type=f32)
    o_ref[...] = _rms(h, gn_ref[...]) if last else h


def _ple(h2, p2, gp, wg, wp, gn, last):
    T, D = h2.shape
    P = p2.shape[1]
    tm = min(512, T)
    full = lambda shape: pl.BlockSpec(shape, lambda i: (0,) * len(shape))
    return pl.pallas_call(
        functools.partial(_ple_body, last=last),
        grid=(T // tm,),
        in_specs=[pl.BlockSpec((tm, D), lambda i: (i, 0)), pl.BlockSpec((tm, P), lambda i: (i, 0)),
                  full((1, D)), full((D, D)), full((P, D)), full((1, D))],
        out_specs=pl.BlockSpec((tm, D), lambda i: (i, 0)),
        out_shape=jax.ShapeDtypeStruct((T, D), f32),
        compiler_params=_params("parallel"),
        name="ple_norm",
    )(h2, p2, gp, wg, wp, gn)


def _overlap_t(S):
    NC, NS = S // CMP_STRIDE, S // SEL_BLOCK
    n = np.arange(NC)[None, :] * CMP_STRIDE
    j = np.arange(NS)[:, None] * SEL_BLOCK
    ov = (n < j + SEL_BLOCK) & (n + CMP_LEN - 1 >= j) & (np.arange(NC)[None, :] < NC - 1)
    return jnp.asarray(ov, bf16)


def _pack_expert_tiles(tab):
    even = jnp.concatenate([tab[:, (2 * s) * LANES:(2 * s + 1) * LANES] for s in range(EXPERT_ROWS)], axis=1)
    odd = jnp.concatenate([tab[:, (2 * s + 1) * LANES:(2 * s + 2) * LANES] for s in range(EXPERT_ROWS)], axis=1)
    bits = lambda a: lax.bitcast_convert_type(a.astype(bf16), jnp.uint16).astype(jnp.uint32)
    words = bits(even) | (bits(odd) << 16)
    return lax.bitcast_convert_type(words, i32).reshape(tab.shape[0] * EXPERT_ROWS, LANES)


def _fold_matrix(hk):
    return jnp.asarray(np.arange(hk * SUBLANES)[:, None] // SUBLANES == np.arange(hk)[None, :], bf16)


def kernel(x, p, positions, norm_mix, w_in, conv_w, conv_b, lru_wr, lru_br, lru_wi, lru_bi, lru_lam,
           cmp_pe_k, cmp_w1_k, cmp_b1_k, cmp_w2_k, cmp_b2_k, cmp_pe_v, cmp_w1_v, cmp_b1_v, cmp_w2_v, cmp_b2_v,
           w_a, w_b, w_out, norm_ffn, peer_wq, peer_keys, peer_u, peer_v, norm_ple, ple_wg, ple_wp,
           norm_final):
    B, S, D = x.shape
    T = B * S
    depth = norm_mix.shape[0]
    HK = PEER_HEADS * PEER_TOPK
    half = HEAD_DIM // 2
    inv = ROPE_THETA ** (-jnp.arange(half, dtype=f32) / half)
    inv2 = jnp.concatenate([inv, inv])[None, :]
    sgn = jnp.concatenate([-jnp.ones((half,), f32), jnp.ones((half,), f32)])[None, :]
    pos = positions.reshape(T, 1)
    ovt = _overlap_t(S)
    fold = _fold_matrix(HK)
    spread = fold.T
    fl = CMP_LEN * HEAD_DIM
    row = lambda v: v.reshape(1, -1)

    h = x.reshape(T, D)
    for i in range(depth):
        w = w_in[i]
        sp = np.cumsum((D_RNN, D_RNN, D_ATTN, 6 * N_KV * HEAD_DIM, 3 * N_HEADS, 2 * D))
        w_ng = jnp.pad(w[:, sp[3]:sp[4]], ((0, 0), (0, NZ - C_NG - 3 * N_HEADS)))
        w_all = jnp.concatenate([w[:, :sp[2]], w[:, sp[4]:], w[:, sp[2]:sp[3]], w_ng], axis=1).astype(bf16)
        z = _inproj(h, row(norm_mix[i]), w_all)

        wri = jnp.concatenate([lru_wr[i], lru_wi[i]], axis=-1).astype(bf16)
        hag = _lru(z, B, S, conv_w[i], row(conv_b[i]), wri, row(lru_br[i]), row(lru_bi[i]), row(lru_lam[i]))

        qn, qr, kc_in, vc_in, ks, vs, kw, vw = _prep(z, pos, inv2, sgn, B, S)
        flat = lambda a: a.reshape(B, N_KV, S // CMP_STRIDE, CMP_STRIDE * HEAD_DIM)
        cmp_params = lambda pe, w1, b1, w2, b2: (
            w1.astype(bf16), jnp.broadcast_to(pe.reshape(1, fl), (SUBLANES, fl)).astype(bf16), row(b1),
            w2.astype(bf16), row(b2))
        kc, vc = _compress(flat(kc_in), flat(vc_in),
                           cmp_params(cmp_pe_k[i], cmp_w1_k[i], cmp_b1_k[i], cmp_w2_k[i], cmp_b2_k[i]),
                           cmp_params(cmp_pe_v[i], cmp_w1_v[i], cmp_b1_v[i], cmp_w2_v[i], cmp_b2_v[i]))
        ob = _nsa(qn, qr, z, kc, vc, ks, vs, kw, vw, ovt, B, S)

        keys = peer_keys[i]
        same_head = jnp.eye(PEER_HEADS, dtype=bool)[:, None, None, :, None, None]
        same_half = jnp.eye(2, dtype=bool)[None, :, None, None, :, None]
        kbd = jnp.where(same_head & same_half, keys[None, :, :, None, None, :], 0.0)
        kbd = kbd.reshape(2 * PEER_HEADS * N_KEYS, PEER_HEADS * PEER_QDIM).astype(bf16)
        h1, n2, e_idx, gates = _post(hag, ob, z, h, w_a[i].astype(bf16), w_b[i].astype(bf16),
                                     w_out[i].astype(bf16), row(norm_ffn[i]), peer_wq[i].astype(bf16), kbd)

        tiles = _pack_expert_tiles
        h2 = _peer(n2.reshape(T, SUBLANES, LANES), e_idx, gates, h1.reshape(T, SUBLANES, LANES),
                   tiles(peer_u[i]), tiles(peer_v[i]), fold, spread)

        h = _ple(h2.reshape(T, D), p[i].reshape(T, -1), row(norm_ple[i]), ple_wg[i].astype(bf16),
                 ple_wp[i].astype(bf16), row(norm_final), last=(i == depth - 1))
    return h.reshape(B, S, D)
```

```python
import functools

import numpy as np
import jax
import jax.numpy as jnp
from jax import lax
from jax.experimental import pallas as pl
from jax.experimental.pallas import tpu as pltpu

f32 = jnp.float32
bf16 = jnp.bfloat16
i32 = jnp.int32

EPS = 1e-6
D_RNN = 1024
RNN_BLOCKS = 8
RNN_BLOCK = D_RNN // RNN_BLOCKS
CONV_W = 4
LRU_C = 8.0
N_HEADS = 8
N_KV = 2
HPG = N_HEADS // N_KV
HEAD_DIM = 128
D_ATTN = N_HEADS * HEAD_DIM
CMP_LEN = 32
CMP_STRIDE = 16
CMP_HID = 256
SEL_BLOCK = 64
SEL_TOPK = 16
WINDOW = 512
ROPE_THETA = 10000.0
PEER_HEADS = 8
N_KEYS = 128
PEER_QDIM = 128
PEER_HALF = PEER_QDIM // 2
PEER_TOPK = 16
NEG_INF = -1e30
FORCE_SCORE = 1e3
LOWEST = -3e38
LOG2_E = 1.4426950408889634

LANES = 128
SUBLANES = 8
VMEM_LIMIT_BYTES = 56 * 1024 * 1024

C_RX, C_RG, C_Q, C_M, C_KV, C_NG = 0, 1024, 2048, 3072, 5120, 6656
NZ = 6912

NT_DIMS = (((1,), (1,)), ((), ()))
TN_DIMS = (((0,), (0,)), ((), ()))


def _params(*sem):
    return pltpu.CompilerParams(dimension_semantics=sem, vmem_limit_bytes=VMEM_LIMIT_BYTES)


def _rms(x, g):
    return x * lax.rsqrt(jnp.mean(x * x, axis=-1, keepdims=True) + EPS) * g


def _split_bf16(x):
    hi = x.astype(bf16)
    return hi, (x - hi.astype(f32)).astype(bf16)


def _inproj_body(x_ref, g_ref, w_ref, o_ref, n_ref):
    @pl.when(pl.program_id(1) == 0)
    def _():
        n_ref[...] = _rms(x_ref[...], g_ref[...]).astype(bf16)

    o_ref[...] = jnp.dot(n_ref[...], w_ref[...], preferred_element_type=f32)


def _inproj(x2, g, w):
    T, D = x2.shape
    tm, tn = min(1024, T), 768
    return pl.pallas_call(
        _inproj_body,
        grid=(T // tm, NZ // tn),
        in_specs=[pl.BlockSpec((tm, D), lambda i, j: (i, 0)),
                  pl.BlockSpec((1, D), lambda i, j: (0, 0)),
                  pl.BlockSpec((D, tn), lambda i, j: (0, j))],
        out_specs=pl.BlockSpec((tm, tn), lambda i, j: (i, j)),
        out_shape=jax.ShapeDtypeStruct((T, NZ), f32),
        scratch_shapes=[pltpu.VMEM((tm, D), bf16)],
        compiler_params=_params("parallel", "arbitrary"),
        name="inproj",
    )(x2, g, w)


def _lru_body(zx_ref, zg_ref, cw_ref, cb_ref, wri_ref, br_ref, bi_ref, lam_ref, o_ref,
              xs_ref, a_ref, u_ref, h_ref):
    ts = zx_ref.shape[0]

    @pl.when(pl.program_id(1) == 0)
    def _():
        xs_ref[0:SUBLANES, :] = jnp.zeros((SUBLANES, D_RNN), f32)
        h_ref[...] = jnp.zeros_like(h_ref)

    xs_ref[SUBLANES:, :] = zx_ref[...]
    xa = cb_ref[...]
    for k in range(CONV_W):
        off = SUBLANES - (CONV_W - 1) + k
        xa = xa + cw_ref[k:k + 1, :] * xs_ref[off:off + ts, :]
    xs_ref[0:SUBLANES, :] = xs_ref[ts:ts + SUBLANES, :]

    sp = jax.nn.softplus(-lam_ref[...])
    xb = xa.astype(bf16)
    for n in range(RNN_BLOCKS):
        sl = slice(n * RNN_BLOCK, (n + 1) * RNN_BLOCK)
        gates = jnp.dot(xb[:, sl], wri_ref[n], preferred_element_type=f32)
        r = jax.nn.sigmoid(gates[:, :RNN_BLOCK] + br_ref[:, sl])
        ig = jax.nn.sigmoid(gates[:, RNN_BLOCK:] + bi_ref[:, sl])
        log_a = -LRU_C * r * sp[:, sl]
        a = jnp.exp(log_a)
        u = jnp.sqrt(-jnp.tanh(log_a) * (a * a + 1.0)) * (ig * xa[:, sl])
        a_ref[:, sl] = a
        u_ref[:, sl] = u

    def step8(j, h):
        base = pl.multiple_of(j * SUBLANES, SUBLANES)
        for k in range(SUBLANES):
            h = a_ref[pl.ds(base + k, 1), :] * h + u_ref[pl.ds(base + k, 1), :]
            u_ref[pl.ds(base + k, 1), :] = h
        return h

    h_ref[0:1, :] = lax.fori_loop(0, ts // SUBLANES, step8, h_ref[0:1, :])
    o_ref[...] = (u_ref[...] * jax.nn.gelu(zg_ref[...])).astype(bf16)


def _lru(z, B, S, cw, cb, wri, br, bi, lam):
    ts = min(512, S)
    nS = S // ts
    vec = lambda: pl.BlockSpec((1, D_RNN), lambda b, i: (0, 0))
    return pl.pallas_call(
        _lru_body,
        grid=(B, nS),
        in_specs=[pl.BlockSpec((ts, D_RNN), lambda b, i: (b * nS + i, C_RX // D_RNN)),
                  pl.BlockSpec((ts, D_RNN), lambda b, i: (b * nS + i, C_RG // D_RNN)),
                  pl.BlockSpec((CONV_W, D_RNN), lambda b, i: (0, 0)), vec(),
                  pl.BlockSpec((RNN_BLOCKS, RNN_BLOCK, 2 * RNN_BLOCK), lambda b, i: (0, 0, 0)),
                  vec(), vec(), vec()],
        out_specs=pl.BlockSpec((ts, D_RNN), lambda b, i: (b * nS + i, 0)),
        out_shape=jax.ShapeDtypeStruct((B * S, D_RNN), bf16),
        scratch_shapes=[pltpu.VMEM((ts + SUBLANES, D_RNN), f32), pltpu.VMEM((ts, D_RNN), f32),
                        pltpu.VMEM((ts, D_RNN), f32), pltpu.VMEM((SUBLANES, D_RNN), f32)],
        compiler_params=_params("parallel", "arbitrary"),
        name="rglru",
    )(z, z, cw, cb, wri, br, bi, lam)


def _prep_body(pos_ref, inv_ref, sgn_ref, zq_ref, zc_ref, zs_ref, zw_ref,
               qn_ref, qr_ref, kc_ref, vc_ref, ks_ref, vs_ref, kw_ref, vw_ref):
    ang = pos_ref[...].astype(f32) * inv_ref[...]
    cs = jnp.cos(ang)
    sn = jnp.sin(ang) * sgn_ref[...]

    def rope(v):
        return v * cs + pltpu.roll(v, HEAD_DIM // 2, axis=1) * sn

    scale = HEAD_DIM ** -0.5 * LOG2_E
    for h in range(N_HEADS):
        sl = slice(h * HEAD_DIM, (h + 1) * HEAD_DIM)
        q = zq_ref[:, sl]
        qn_ref[:, sl] = (q * scale).astype(bf16)
        qr_ref[:, sl] = (rope(q) * scale).astype(bf16)
    for g in range(N_KV):
        ksl = slice(g * HEAD_DIM, (g + 1) * HEAD_DIM)
        vsl = slice((N_KV + g) * HEAD_DIM, (N_KV + g + 1) * HEAD_DIM)
        kc_ref[g] = zc_ref[:, ksl].astype(bf16)
        vc_ref[g] = zc_ref[:, vsl].astype(bf16)
        ks_ref[g] = rope(zs_ref[:, ksl]).astype(bf16)
        vs_ref[g] = zs_ref[:, vsl].astype(bf16)
        kw_ref[g] = rope(zw_ref[:, ksl]).astype(bf16)
        vw_ref[g] = zw_ref[:, vsl].astype(bf16)


def _prep(z, pos, inv2, sgn, B, S):
    T = B * S
    tp = min(256, S)
    nS = S // tp
    kvw = 2 * N_KV * HEAD_DIM
    row = lambda b, i: b * nS + i
    kv_out = pl.BlockSpec((None, N_KV, tp, HEAD_DIM), lambda b, i: (b, 0, i, 0))
    kv_shape = jax.ShapeDtypeStruct((B, N_KV, S, HEAD_DIM), bf16)
    q_shape = jax.ShapeDtypeStruct((T, D_ATTN), bf16)
    return pl.pallas_call(
        _prep_body,
        grid=(B, nS),
        in_specs=[pl.BlockSpec((tp, 1), lambda b, i: (row(b, i), 0)),
                  pl.BlockSpec((1, HEAD_DIM), lambda b, i: (0, 0)),
                  pl.BlockSpec((1, HEAD_DIM), lambda b, i: (0, 0)),
                  pl.BlockSpec((tp, D_ATTN), lambda b, i: (row(b, i), C_Q // D_ATTN)),
                  pl.BlockSpec((tp, kvw), lambda b, i: (row(b, i), C_KV // kvw)),
                  pl.BlockSpec((tp, kvw), lambda b, i: (row(b, i), C_KV // kvw + 1)),
                  pl.BlockSpec((tp, kvw), lambda b, i: (row(b, i), C_KV // kvw + 2))],
        out_specs=[pl.BlockSpec((tp, D_ATTN), lambda b, i: (row(b, i), 0)),
                   pl.BlockSpec((tp, D_ATTN), lambda b, i: (row(b, i), 0))] + [kv_out] * 6,
        out_shape=[q_shape, q_shape] + [kv_shape] * 6,
        compiler_params=_params("parallel", "parallel"),
        name="attn_prep",
    )(pos, inv2, sgn, z, z, z, z)


def _cmp_body(kf_ref, vf_ref, w1k_ref, pek_ref, b1k_ref, w2k_ref, b2k_ref,
              w1v_ref, pev_ref, b1v_ref, w2v_ref, b2v_ref, kc_ref, vc_ref):
    nc = kf_ref.shape[0]
    half = CMP_STRIDE * HEAD_DIM

    def mlp(f_ref, w1_ref, pe_ref, b1_ref, w2_ref, b2_ref):
        f = f_ref[...]
        first = jnp.dot(f, w1_ref[0:half, :], preferred_element_type=f32)
        second = jnp.dot(f, w1_ref[half:2 * half, :], preferred_element_type=f32)
        pew = jnp.dot(pe_ref[...], w1_ref[...], preferred_element_type=f32)[0:1]
        hid = jax.nn.gelu(first + pltpu.roll(second, nc - 1, axis=0) + pew + b1_ref[...])
        return jnp.dot(hid.astype(bf16), w2_ref[...], preferred_element_type=f32) + b2_ref[...]

    kc_ref[...] = mlp(kf_ref, w1k_ref, pek_ref, b1k_ref, w2k_ref, b2k_ref)
    vc_ref[...] = mlp(vf_ref, w1v_ref, pev_ref, b1v_ref, w2v_ref, b2v_ref)


def _compress(kflat, vflat, pk, pv):
    B, G, NC, F = kflat.shape
    fl = CMP_LEN * HEAD_DIM
    flat = pl.BlockSpec((None, None, NC, F), lambda b, g: (b, g, 0, 0))
    full = lambda shape: pl.BlockSpec(shape, lambda b, g: (0,) * len(shape))
    wspecs = [full((fl, CMP_HID)), full((SUBLANES, fl)), full((1, CMP_HID)),
              full((CMP_HID, HEAD_DIM)), full((1, HEAD_DIM))]
    out = pl.BlockSpec((None, None, NC, HEAD_DIM), lambda b, g: (b, g, 0, 0))
    shape = jax.ShapeDtypeStruct((B, G, NC, HEAD_DIM), f32)
    return pl.pallas_call(
        _cmp_body,
        grid=(B, G),
        in_specs=[flat, flat] + wspecs + wspecs,
        out_specs=[out, out],
        out_shape=[shape, shape],
        compiler_params=_params("parallel", "parallel"),
        name="compress",
    )(kflat, vflat, *pk, *pv)


def _softmax0(s):
    e = jnp.exp2(s - jnp.max(s, axis=0, keepdims=True))
    return e * (1.0 / jnp.sum(e, axis=0, keepdims=True))


def _nsa_body(qn_ref, qr_ref, gz_ref, kc_ref, vc_ref, ks_ref, vs_ref, kw_ref, vw_ref, ovt_ref,
              o_ref, sel_ref, gt_ref, qrs_ref, part_ref, m_ref, l_ref, acc_ref, sa_ref, sb_ref,
              *, S, tq, top_n, ck):
    g = pl.program_id(1)
    t0 = pl.program_id(2) * tq
    NC, NS = S // CMP_STRIDE, S // SEL_BLOCK
    heads = [slice(h * HEAD_DIM, (h + 1) * HEAD_DIM) for h in range(HPG)]
    trow = t0 + lax.broadcasted_iota(i32, (1, tq), 1)

    tile4 = lambda a: jnp.concatenate([a] * HPG, axis=1)
    qn = jnp.concatenate([qn_ref[:, hs] for hs in heads], axis=0)
    qrs_ref[...] = jnp.concatenate([qr_ref[:, hs] for hs in heads], axis=0)

    gt_ref[...] = jax.nn.sigmoid(gz_ref[...]).T

    def gate(branch):
        return jnp.concatenate([gt_ref[pl.ds(g * (HPG * 3) + h * 3 + branch, 1), :] for h in range(HPG)], axis=1)

    cmp_end = lax.broadcasted_iota(i32, (NC, tq), 0) * CMP_STRIDE + (CMP_LEN - 1)
    cbias = jnp.where(cmp_end <= trow, 0.0, NEG_INF)
    live = (trow >= CMP_LEN - 1).astype(f32)
    sc = lax.dot_general(kc_ref[...].astype(bf16), qn, NT_DIMS, preferred_element_type=f32)
    pc = _softmax0(sc + tile4(cbias)) * tile4(live)
    ocT = lax.dot_general(vc_ref[...].astype(bf16), pc.astype(bf16), TN_DIMS, preferred_element_type=f32)
    ps = pc[:, 0:tq]
    for h in range(1, HPG):
        ps = ps + pc[:, h * tq:(h + 1) * tq]

    wl = WINDOW + tq
    start = pl.multiple_of(jnp.maximum(t0 - WINDOW, 0), tq)
    dist = trow - (start + lax.broadcasted_iota(i32, (wl, tq), 0))
    wbias = jnp.where(dist >= 0, jnp.where(dist < WINDOW, 0.0, NEG_INF), NEG_INF)
    sw = lax.dot_general(kw_ref[pl.ds(start, wl), :], qrs_ref[...], NT_DIMS, preferred_element_type=f32)
    pw = _softmax0(sw + tile4(wbias)).astype(bf16)
    owT = lax.dot_general(vw_ref[pl.ds(start, wl), :], pw, TN_DIMS, preferred_element_type=f32)
    part_ref[...] = gate(0) * ocT + gate(2) * owT

    ps_hi, ps_lo = _split_bf16(ps)
    imp = (jnp.dot(ovt_ref[...], ps_hi, preferred_element_type=f32)
           + jnp.dot(ovt_ref[...], ps_lo, preferred_element_type=f32))
    j = lax.broadcasted_iota(i32, (NS, tq), 0)
    t = t0 + lax.broadcasted_iota(i32, (NS, tq), 1)
    cur = t >> (SEL_BLOCK.bit_length() - 1)
    forced = (j == 0) | (j == cur) | (j == cur - 1)
    imp = jnp.where(forced, FORCE_SCORE, imp)
    imp = jnp.where(j * SEL_BLOCK <= t, imp, NEG_INF)
    sel = jnp.zeros((NS, tq), f32)
    for _ in range(top_n):
        m = jnp.max(imp, axis=0, keepdims=True)
        first = jnp.min(jnp.where(imp == m, j, NS), axis=0, keepdims=True)
        hit = j == first
        sel = jnp.where(hit & (m > 0.5 * NEG_INF), 1.0, sel)
        imp = jnp.where(hit, LOWEST, imp)
    sel_ref[...] = sel

    r64 = lax.broadcasted_iota(i32, (SEL_BLOCK, tq), 0)
    bpc = ck // SEL_BLOCK

    nch = (t0 + tq + ck - 1) // ck
    last = S // ck - 1

    def scores(c):
        base = pl.multiple_of(jnp.minimum(c, last) * ck, ck)
        return lax.dot_general(ks_ref[pl.ds(base, ck), :], qrs_ref[...], NT_DIMS, preferred_element_type=f32)

    def attend(c, s_ref):
        base = pl.multiple_of(c * ck, ck)
        bias = []
        for r in range(bpc):
            srow = sel_ref[pl.ds(c * bpc + r, 1), :]
            lim = jnp.where(srow > 0.5, trow, -1) - (base + SEL_BLOCK * r)
            bias.append(jnp.where(r64 <= lim, 0.0, NEG_INF))
        s = s_ref[...] + tile4(jnp.concatenate(bias, axis=0))
        m = m_ref[0:1, :]
        m_new = jnp.maximum(m, jnp.max(s, axis=0, keepdims=True))
        alpha = jnp.exp2(m - m_new)
        p = jnp.exp2(s - m_new)
        l_ref[0:1, :] = alpha * l_ref[0:1, :] + jnp.sum(p, axis=0, keepdims=True)
        pv = lax.dot_general(vs_ref[pl.ds(base, ck), :], p.astype(bf16), TN_DIMS, preferred_element_type=f32)
        acc_ref[...] = alpha * acc_ref[...] + pv
        m_ref[0:1, :] = m_new

    m_ref[...] = jnp.full_like(m_ref, NEG_INF)
    l_ref[...] = jnp.zeros_like(l_ref)
    acc_ref[...] = jnp.zeros_like(acc_ref)
    sa_ref[...] = scores(0)

    def pair(i, _):
        sb_ref[...] = scores(2 * i + 1)
        attend(2 * i, sa_ref)
        sa_ref[...] = scores(2 * i + 2)
        attend(2 * i + 1, sb_ref)
        return 0

    lax.fori_loop(0, nch // 2, pair, 0)

    @pl.when(nch % 2 == 1)
    def _():
        attend(nch - 1, sa_ref)

    oT = part_ref[...] + gate(1) * (acc_ref[...] * (1.0 / l_ref[0:1, :]))
    for h, hs in enumerate(heads):
        o_ref[:, hs] = oT[:, h * tq:(h + 1) * tq].T.astype(bf16)


def _nsa(qn, qr, z, kc, vc, ks, vs, kw, vw, ovt, B, S):
    tq, ck = 128, 512
    assert S % ck == 0 and S >= WINDOW + tq
    nS = S // tq
    NC, NS = S // CMP_STRIDE, S // SEL_BLOCK
    gw = HPG * HEAD_DIM
    qspec = pl.BlockSpec((tq, gw), lambda b, g, i: (b * nS + i, g))
    seq = lambda n: pl.BlockSpec((None, None, n, HEAD_DIM), lambda b, g, i: (b, g, 0, 0))
    body = functools.partial(_nsa_body, S=S, tq=tq, top_n=min(SEL_TOPK, NS), ck=ck)
    return pl.pallas_call(
        body,
        grid=(B, N_KV, nS),
        in_specs=[qspec, qspec,
                  pl.BlockSpec((tq, LANES), lambda b, g, i: (b * nS + i, C_NG // LANES)),
                  seq(NC), seq(NC), seq(S), seq(S), seq(S), seq(S),
                  pl.BlockSpec((NS, NC), lambda b, g, i: (0, 0))],
        out_specs=pl.BlockSpec((tq, gw), lambda b, g, i: (b * nS + i, g)),
        out_shape=jax.ShapeDtypeStruct((B * S, D_ATTN), bf16),
        scratch_shapes=[pltpu.VMEM((NS, tq), f32), pltpu.VMEM((LANES, tq), f32),
                        pltpu.VMEM((HPG * tq, HEAD_DIM), bf16), pltpu.VMEM((HEAD_DIM, HPG * tq), f32),
                        pltpu.VMEM((SUBLANES, HPG * tq), f32), pltpu.VMEM((SUBLANES, HPG * tq), f32),
                        pltpu.VMEM((HEAD_DIM, HPG * tq), f32),
                        pltpu.VMEM((ck, HPG * tq), f32), pltpu.VMEM((ck, HPG * tq), f32)],
        compiler_params=_params("parallel", "parallel", "arbitrary"),
        name="nsa",
    )(qn, qr, z, kc, vc, ks, vs, kw, vw, ovt)


def _extract_max(x, tag, invalid):
    m = jnp.max(x, axis=0, keepdims=True)
    first = jnp.min(jnp.where(x == m, tag, invalid), axis=0, keepdims=True)
    return m, first


def _oddeven_mergesort_pairs(n):
    pairs = []
    p = 1
    while p < n:
        k = p
        while k >= 1:
            for j in range(k % p, n - k, 2 * k):
                for i in range(min(k, n - j - k)):
                    if (i + j) // (2 * p) == (i + j + k) // (2 * p):
                        pairs.append((i + j, i + j + k))
            k //= 2
        p *= 2
    return pairs


def _compare_exchange(v, p, i, j):
    swap = v[j] > v[i]
    v[i], v[j] = jnp.where(swap, v[j], v[i]), jnp.where(swap, v[i], v[j])
    p[i], p[j] = jnp.where(swap, p[j], p[i]), jnp.where(swap, p[i], p[j])


def _sorted_top16(x):
    n = PEER_TOPK
    sub = lax.broadcasted_iota(i32, (SUBLANES, x.shape[1]), 0)
    v = [x[r * SUBLANES:(r + 1) * SUBLANES, :] for r in range(n)]
    p = [sub + r * SUBLANES for r in range(n)]
    for i, j in _oddeven_mergesort_pairs(n):
        _compare_exchange(v, p, i, j)
    dropped = jnp.full_like(v[0], LOWEST)
    for shift in (4, 2, 1):
        bv = [pltpu.roll(a, shift, axis=0) for a in v]
        bp = [pltpu.roll(a, shift, axis=0) for a in p]
        dropped = jnp.maximum(dropped, pltpu.roll(dropped, shift, axis=0))
        for r in range(n):
            take = bv[n - 1 - r] > v[r]
            dropped = jnp.maximum(dropped, jnp.where(take, v[r], bv[n - 1 - r]))
            v[r] = jnp.where(take, bv[n - 1 - r], v[r])
            p[r] = jnp.where(take, bp[n - 1 - r], p[r])
        d = n // 2
        while d >= 1:
            for i in range(n):
                if i & d == 0:
                    _compare_exchange(v, p, i, i + d)
            d //= 2
    tie = v[n - 1] == dropped
    for k in range(n - 1):
        tie = tie | (v[k] == v[k + 1])
    tie = jnp.where(tie, 1.0, 0.0)[0:1, :]
    return [a[0:1, :] for a in v], [a[0:1, :] for a in p], tie


def _post_body(ha_ref, ob_ref, zm0_ref, zm1_ref, x_ref, wa_ref, wb_ref, wo_ref, gf_ref, wq_ref, kbd_ref,
               h_ref, n_ref, e_ref, g_ref, st_ref, va_ref, ia_ref, et_ref, gt_ref):
    tm = x_ref.shape[0]
    K = PEER_TOPK
    ya = jnp.dot(ha_ref[...], wa_ref[...], preferred_element_type=f32)
    yb = jnp.dot(ob_ref[...], wb_ref[...], preferred_element_type=f32)
    merged = jax.nn.sigmoid(zm0_ref[...]) * ya + jax.nn.sigmoid(zm1_ref[...]) * yb
    h = x_ref[...] + jnp.dot(merged.astype(bf16), wo_ref[...], preferred_element_type=f32)
    h_ref[...] = h
    n = _rms(h, gf_ref[...]).astype(bf16)
    n_ref[...] = n
    q = jnp.dot(n, wq_ref[...], preferred_element_type=f32).astype(bf16)
    st_ref[...] = lax.dot_general(kbd_ref[...], q, NT_DIMS, preferred_element_type=f32)

    ncols = tm // LANES
    rows = lax.broadcasted_iota(i32, (N_KEYS, LANES), 0)

    def top_half(hd, _):
        chains = [(2 * hd + half, c) for half in range(2) for c in range(ncols)]
        scores = lambda hc, c: st_ref[pl.ds(pl.multiple_of(hc * N_KEYS, N_KEYS), N_KEYS), c * LANES:(c + 1) * LANES]
        tied = jnp.zeros((1, LANES), f32)
        for hc, c in chains:
            vals, idxs, tie = _sorted_top16(scores(hc, c))
            for k in range(K):
                va_ref[c, pl.ds(hc * K + k, 1), :] = vals[k]
                ia_ref[c, pl.ds(hc * K + k, 1), :] = idxs[k]
            tied = jnp.maximum(tied, tie)

        @pl.when(jnp.max(tied) > 0.0)
        def _():
            ys = [scores(hc, c) for hc, c in chains]
            for k in range(K):
                for n, (hc, c) in enumerate(chains):
                    m, first = _extract_max(ys[n], rows, N_KEYS)
                    va_ref[c, pl.ds(hc * K + k, 1), :] = m
                    ia_ref[c, pl.ds(hc * K + k, 1), :] = first
                    ys[n] = jnp.where(rows == first, LOWEST, ys[n])

        return 0

    lax.fori_loop(0, PEER_HEADS, top_half, 0)

    r8 = lax.broadcasted_iota(i32, (SUBLANES, LANES), 0)

    def combine(hd, _):
        for c in range(ncols):
            combine_block(hd, c)
        return 0

    def combine_block(hd, c):
        b1 = pl.multiple_of(hd * 2 * K, 2 * K)
        v1, v2 = va_ref[c, pl.ds(b1, K), :], va_ref[c, pl.ds(b1 + K, K), :]
        i1, i2 = ia_ref[c, pl.ds(b1, K), :], ia_ref[c, pl.ds(b1 + K, K), :]
        cand = [v1[0:1] + v2]
        tag = [lax.broadcasted_iota(i32, (K, LANES), 0)]
        eid = [(i1[0:1] * N_KEYS + i2) * EXPERT_ROWS]
        for a in range(1, SUBLANES):
            ok = r8 < K // (a + 1)
            cand.append(jnp.where(ok, v1[a:a + 1] + v2[0:SUBLANES], LOWEST))
            tag.append(a * K + r8)
            eid.append((i1[a:a + 1] * N_KEYS + i2[0:SUBLANES]) * EXPERT_ROWS)
        cand.append(v1[SUBLANES:K] + v2[0:1])
        tag.append((SUBLANES + r8) * K)
        eid.append((i1[SUBLANES:K] * N_KEYS + i2[0:1]) * EXPERT_ROWS)
        cand, tag, eid = (jnp.concatenate(v, axis=0) for v in (cand, tag, eid))
        tag = tag * TAG_SHIFT + eid
        best0 = None
        den = jnp.zeros((1, LANES), f32)
        for k in range(K):
            m, first = _extract_max(cand, tag, K * K * TAG_SHIFT)
            hit = tag == first
            et_ref[c, pl.ds(hd * K + k, 1), :] = first & (TAG_SHIFT - 1)
            cand = jnp.where(hit, LOWEST, cand)
            best0 = m if best0 is None else best0
            ex = jnp.exp(m - best0)
            gt_ref[c, pl.ds(hd * K + k, 1), :] = ex
            den = den + ex
        gt_ref[c, pl.ds(hd * K, K), :] = gt_ref[c, pl.ds(hd * K, K), :] * (1.0 / den)

    lax.fori_loop(0, PEER_HEADS, combine, 0)
    for c in range(ncols):
        e_ref[c * LANES:(c + 1) * LANES, :] = et_ref[c].T
        g_ref[c * LANES:(c + 1) * LANES, :] = gt_ref[c].T


def _post(hag, ob, z, x2, wa, wb, wo, gf, wq, kbd):
    T, D = x2.shape
    tm = min(256, T)
    HK = PEER_HEADS * PEER_TOPK
    rowblk = lambda w, c=0: pl.BlockSpec((tm, w), lambda i: (i, c))
    full = lambda shape: pl.BlockSpec(shape, lambda i: (0,) * len(shape))
    return pl.pallas_call(
        _post_body,
        grid=(T // tm,),
        in_specs=[rowblk(D), rowblk(D), rowblk(D, C_M // D), rowblk(D, C_M // D + 1), rowblk(D),
                  full((D, D)), full((D, D)), full((D, D)), full((1, D)), full((D, D)),
                  full((2 * PEER_HEADS * N_KEYS, D))],
        out_specs=[rowblk(D), rowblk(D), rowblk(HK), rowblk(HK)],
        out_shape=[jax.ShapeDtypeStruct((T, D), f32), jax.ShapeDtypeStruct((T, D), bf16),
                   jax.ShapeDtypeStruct((T, HK), i32), jax.ShapeDtypeStruct((T, HK), f32)],
        scratch_shapes=[pltpu.VMEM((2 * PEER_HEADS * N_KEYS, tm), f32),
                        pltpu.VMEM((tm // LANES, 2 * PEER_HEADS * PEER_TOPK, LANES), f32),
                        pltpu.VMEM((tm // LANES, 2 * PEER_HEADS * PEER_TOPK, LANES), i32),
                        pltpu.VMEM((tm // LANES, HK, LANES), i32), pltpu.VMEM((tm // LANES, HK, LANES), f32)],
        compiler_params=_params("parallel"),
        name="post_route",
    )(hag, ob, z, z, x2, wa, wb, wo, gf, wq, kbd)


PEER_GROUP = 32
EXPERT_ROWS = SUBLANES // 2
TAG_SHIFT = N_KEYS * N_KEYS * EXPERT_ROWS


def _diag_mask(width):
    lane = lax.broadcasted_iota(i32, (SUBLANES, width), 1)
    sub = lax.broadcasted_iota(i32, (SUBLANES, width), 0)
    return (lane & (SUBLANES - 1)) == sub


def _gather_tiles(idx_ref, tab_ref, t):
    tiles = []
    for j in range(idx_ref.shape[1]):
        r = pl.multiple_of(idx_ref[t, j], EXPERT_ROWS)
        tiles.append(pltpu.bitcast(tab_ref[pl.ds(r, EXPERT_ROWS), :], bf16))
    return jnp.concatenate(tiles, axis=0)


def _peer_u_body(idx_ref, x_ref, g_ref, tab_ref, fold_ref, act_ref, y_ref):
    tt, hk = g_ref.shape
    diag = _diag_mask(hk * SUBLANES)

    def group(gi, _):
        for k in range(PEER_GROUP):
            t = gi * PEER_GROUP + k
            y = lax.dot_general(x_ref[t], _gather_tiles(idx_ref, tab_ref, t), NT_DIMS, preferred_element_type=f32)
            y_ref[pl.ds(t, 1), :] = jnp.sum(jnp.where(diag, y, 0.0), axis=0, keepdims=True)
        return 0

    lax.fori_loop(0, tt // PEER_GROUP, group, 0)
    y_hi, y_lo = _split_bf16(y_ref[...])
    s = (jnp.dot(y_hi, fold_ref[...], preferred_element_type=f32)
         + jnp.dot(y_lo, fold_ref[...], preferred_element_type=f32))
    act_ref[...] = jax.nn.gelu(s) * g_ref[...]


def _peer_v_body(idx_ref, act_ref, h_ref, tab_ref, spread_ref, o_ref, a_ref):
    tt, hk = act_ref.shape
    diag = _diag_mask(hk * SUBLANES)
    a_hi, a_lo = _split_bf16(act_ref[...])
    a_ref[...] = (jnp.dot(a_hi, spread_ref[...], preferred_element_type=f32)
                  + jnp.dot(a_lo, spread_ref[...], preferred_element_type=f32))

    def group(gi, _):
        for k in range(PEER_GROUP):
            t = gi * PEER_GROUP + k
            a8 = jnp.where(diag, jnp.broadcast_to(a_ref[pl.ds(t, 1), :], diag.shape), 0.0)
            lhs = jnp.concatenate(_split_bf16(a8), axis=0)
            out = jnp.dot(lhs, _gather_tiles(idx_ref, tab_ref, t), preferred_element_type=f32)
            o_ref[t] = h_ref[t] + out[0:SUBLANES] + out[SUBLANES:2 * SUBLANES]
        return 0

    lax.fori_loop(0, tt // PEER_GROUP, group, 0)


def _peer(n3, e_idx, gates, h3, u_tiles, v_tiles, fold, spread):
    T = n3.shape[0]
    HK = e_idx.shape[1]
    tt = min(256, T)
    smem_idx = pl.BlockSpec((tt, HK), lambda i: (i, 0), memory_space=pltpu.SMEM)
    tok = pl.BlockSpec((tt, SUBLANES, LANES), lambda i: (i, 0, 0))
    row = pl.BlockSpec((tt, HK), lambda i: (i, 0))
    table = pl.BlockSpec(memory_space=pltpu.VMEM)
    act = pl.pallas_call(
        _peer_u_body,
        grid=(T // tt,),
        in_specs=[smem_idx, tok, row, table, pl.BlockSpec(fold.shape, lambda i: (0, 0))],
        out_specs=row,
        out_shape=jax.ShapeDtypeStruct((T, HK), f32),
        scratch_shapes=[pltpu.VMEM((tt, HK * SUBLANES), f32)],
        compiler_params=_params("parallel"),
        name="peer_u",
    )(e_idx, n3, gates, u_tiles, fold)
    return pl.pallas_call(
        _peer_v_body,
        grid=(T // tt,),
        in_specs=[smem_idx, row, tok, table, pl.BlockSpec(spread.shape, lambda i: (0, 0))],
        out_specs=tok,
        out_shape=jax.ShapeDtypeStruct(h3.shape, f32),
        scratch_shapes=[pltpu.VMEM((tt, HK * SUBLANES), f32)],
        compiler_params=_params("parallel"),
        name="peer_v",
    )(e_idx, act, h3, v_tiles, spread)


def _ple_body(h_ref, p_ref, gp_ref, wg_ref, wp_ref, gn_ref, o_ref, *, last):
    h = h_ref[...]
    gate = jax.nn.sigmoid(jnp.dot(_rms(h, gp_ref[...]).astype(bf16), wg_ref[...], preferred_element_type=f32))
    h = h + gate * jnp.dot(p_ref[...].astype(bf16), wp_ref[...], preferred_element_type=f32)
    o_ref[...] = _rms(h, gn_ref[...]) if last else h


def _ple(h2, p2, gp, wg, wp, gn, last):
    T, D = h2.shape
    P = p2.shape[1]
    tm = min(512, T)
    full = lambda shape: pl.BlockSpec(shape, lambda i: (0,) * len(shape))
    return pl.pallas_call(
        functools.partial(_ple_body, last=last),
        grid=(T // tm,),
        in_specs=[pl.BlockSpec((tm, D), lambda i: (i, 0)), pl.BlockSpec((tm, P), lambda i: (i, 0)),
                  full((1, D)), full((D, D)), full((P, D)), full((1, D))],
        out_specs=pl.BlockSpec((tm, D), lambda i: (i, 0)),
        out_shape=jax.ShapeDtypeStruct((T, D), f32),
        compiler_params=_params("parallel"),
        name="ple_norm",
    )(h2, p2, gp, wg, wp, gn)


def _overlap_t(S):
    NC, NS = S // CMP_STRIDE, S // SEL_BLOCK
    n = np.arange(NC)[None, :] * CMP_STRIDE
    j = np.arange(NS)[:, None] * SEL_BLOCK
    ov = (n < j + SEL_BLOCK) & (n + CMP_LEN - 1 >= j) & (np.arange(NC)[None, :] < NC - 1)
    return jnp.asarray(ov, bf16)


def _pack_expert_tiles(tab):
    even = jnp.concatenate([tab[:, (2 * s) * LANES:(2 * s + 1) * LANES] for s in range(EXPERT_ROWS)], axis=1)
    odd = jnp.concatenate([tab[:, (2 * s + 1) * LANES:(2 * s + 2) * LANES] for s in range(EXPERT_ROWS)], axis=1)
    bits = lambda a: lax.bitcast_convert_type(a.astype(bf16), jnp.uint16).astype(jnp.uint32)
    words = bits(even) | (bits(odd) << 16)
    return lax.bitcast_convert_type(words, i32).reshape(tab.shape[0] * EXPERT_ROWS, LANES)


def _fold_matrix(hk):
    return jnp.asarray(np.arange(hk * SUBLANES)[:, None] // SUBLANES == np.arange(hk)[None, :], bf16)


def kernel(x, p, positions, norm_mix, w_in, conv_w, conv_b, lru_wr, lru_br, lru_wi, lru_bi, lru_lam,
           cmp_pe_k, cmp_w1_k, cmp_b1_k, cmp_w2_k, cmp_b2_k, cmp_pe_v, cmp_w1_v, cmp_b1_v, cmp_w2_v, cmp_b2_v,
           w_a, w_b, w_out, norm_ffn, peer_wq, peer_keys, peer_u, peer_v, norm_ple, ple_wg, ple_wp,
           norm_final):
    B, S, D = x.shape
    T = B * S
    depth = norm_mix.shape[0]
    HK = PEER_HEADS * PEER_TOPK
    half = HEAD_DIM // 2
    inv = ROPE_THETA ** (-jnp.arange(half, dtype=f32) / half)
    inv2 = jnp.concatenate([inv, inv])[None, :]
    sgn = jnp.concatenate([-jnp.ones((half,), f32), jnp.ones((half,), f32)])[None, :]
    pos = positions.reshape(T, 1)
    ovt = _overlap_t(S)
    fold = _fold_matrix(HK)
    spread = fold.T
    fl = CMP_LEN * HEAD_DIM
    row = lambda v: v.reshape(1, -1)

    h = x.reshape(T, D)
    for i in range(depth):
        w = w_in[i]
        sp = np.cumsum((D_RNN, D_RNN, D_ATTN, 6 * N_KV * HEAD_DIM, 3 * N_HEADS, 2 * D))
        w_ng = jnp.pad(w[:, sp[3]:sp[4]], ((0, 0), (0, NZ - C_NG - 3 * N_HEADS)))
        w_all = jnp.concatenate([w[:, :sp[2]], w[:, sp[4]:], w[:, sp[2]:sp[3]], w_ng], axis=1).astype(bf16)
        z = _inproj(h, row(norm_mix[i]), w_all)

        wri = jnp.concatenate([lru_wr[i], lru_wi[i]], axis=-1).astype(bf16)
        hag = _lru(z, B, S, conv_w[i], row(conv_b[i]), wri, row(lru_br[i]), row(lru_bi[i]), row(lru_lam[i]))

        qn, qr, kc_in, vc_in, ks, vs, kw, vw = _prep(z, pos, inv2, sgn, B, S)
        flat = lambda a: a.reshape(B, N_KV, S // CMP_STRIDE, CMP_STRIDE * HEAD_DIM)
        cmp_params = lambda pe, w1, b1, w2, b2: (
            w1.astype(bf16), jnp.broadcast_to(pe.reshape(1, fl), (SUBLANES, fl)).astype(bf16), row(b1),
            w2.astype(bf16), row(b2))
        kc, vc = _compress(flat(kc_in), flat(vc_in),
                           cmp_params(cmp_pe_k[i], cmp_w1_k[i], cmp_b1_k[i], cmp_w2_k[i], cmp_b2_k[i]),
                           cmp_params(cmp_pe_v[i], cmp_w1_v[i], cmp_b1_v[i], cmp_w2_v[i], cmp_b2_v[i]))
        ob = _nsa(qn, qr, z, kc, vc, ks, vs, kw, vw, ovt, B, S)

        keys = peer_keys[i]
        same_head = jnp.eye(PEER_HEADS, dtype=bool)[:, None, None, :, None, None]
        same_half = jnp.eye(2, dtype=bool)[None, :, None, None, :, None]
        kbd = jnp.where(same_head & same_half, keys[None, :, :, None, None, :], 0.0)
        kbd = kbd.reshape(2 * PEER_HEADS * N_KEYS, PEER_HEADS * PEER_QDIM).astype(bf16)
        h1, n2, e_idx, gates = _post(hag, ob, z, h, w_a[i].astype(bf16), w_b[i].astype(bf16),
                                     w_out[i].astype(bf16), row(norm_ffn[i]), peer_wq[i].astype(bf16), kbd)

        tiles = _pack_expert_tiles
        h2 = _peer(n2.reshape(T, SUBLANES, LANES), e_idx, gates, h1.reshape(T, SUBLANES, LANES),
                   tiles(peer_u[i]), tiles(peer_v[i]), fold, spread)

        h = _ple(h2.reshape(T, D), p[i].reshape(T, -1), row(norm_ple[i]), ple_wg[i].astype(bf16),
                 ple_wp[i].astype(bf16), row(norm_final), last=(i == depth - 1))
    return h.reshape(B, S, D)
```

```python
import functools

import numpy as np
import jax
import jax.numpy as jnp
from jax import lax
from jax.experimental import pallas as pl
from jax.experimental.pallas import tpu as pltpu

f32 = jnp.float32
bf16 = jnp.bfloat16
i32 = jnp.int32

EPS = 1e-6
D_RNN = 1024
RNN_BLOCKS = 8
RNN_BLOCK = D_RNN // RNN_BLOCKS
CONV_W = 4
LRU_C = 8.0
N_HEADS = 8
N_KV = 2
HPG = N_HEADS // N_KV
HEAD_DIM = 128
D_ATTN = N_HEADS * HEAD_DIM
CMP_LEN = 32
CMP_STRIDE = 16
CMP_HID = 256
SEL_BLOCK = 64
SEL_TOPK = 16
WINDOW = 512
ROPE_THETA = 10000.0
PEER_HEADS = 8
N_KEYS = 128
PEER_QDIM = 128
PEER_HALF = PEER_QDIM // 2
PEER_TOPK = 16
NEG_INF = -1e30
FORCE_SCORE = 1e3
LOWEST = -3e38
LOG2_E = 1.4426950408889634

LANES = 128
SUBLANES = 8
VMEM_LIMIT_BYTES = 56 * 1024 * 1024

C_RX, C_RG, C_Q, C_M, C_KV, C_NG = 0, 1024, 2048, 3072, 5120, 6656
NZ = 6912

NT_DIMS = (((1,), (1,)), ((), ()))
TN_DIMS = (((0,), (0,)), ((), ()))


def _params(*sem):
    return pltpu.CompilerParams(dimension_semantics=sem, vmem_limit_bytes=VMEM_LIMIT_BYTES)


def _rms(x, g):
    return x * lax.rsqrt(jnp.mean(x * x, axis=-1, keepdims=True) + EPS) * g


def _split_bf16(x):
    hi = x.astype(bf16)
    return hi, (x - hi.astype(f32)).astype(bf16)


def _inproj_body(x_ref, g_ref, w_ref, o_ref, n_ref):
    @pl.when(pl.program_id(1) == 0)
    def _():
        n_ref[...] = _rms(x_ref[...], g_ref[...]).astype(bf16)

    o_ref[...] = jnp.dot(n_ref[...], w_ref[...], preferred_element_type=f32)


def _inproj(x2, g, w):
    T, D = x2.shape
    tm, tn = min(1024, T), 768
    return pl.pallas_call(
        _inproj_body,
        grid=(T // tm, NZ // tn),
        in_specs=[pl.BlockSpec((tm, D), lambda i, j: (i, 0)),
                  pl.BlockSpec((1, D), lambda i, j: (0, 0)),
                  pl.BlockSpec((D, tn), lambda i, j: (0, j))],
        out_specs=pl.BlockSpec((tm, tn), lambda i, j: (i, j)),
        out_shape=jax.ShapeDtypeStruct((T, NZ), f32),
        scratch_shapes=[pltpu.VMEM((tm, D), bf16)],
        compiler_params=_params("parallel", "arbitrary"),
        name="inproj",
    )(x2, g, w)


def _lru_body(zx_ref, zg_ref, cw_ref, cb_ref, wri_ref, br_ref, bi_ref, lam_ref, o_ref,
              xs_ref, a_ref, u_ref, h_ref):
    ts = zx_ref.shape[0]

    @pl.when(pl.program_id(1) == 0)
    def _():
        xs_ref[0:SUBLANES, :] = jnp.zeros((SUBLANES, D_RNN), f32)
        h_ref[...] = jnp.zeros_like(h_ref)

    xs_ref[SUBLANES:, :] = zx_ref[...]
    xa = cb_ref[...]
    for k in range(CONV_W):
        off = SUBLANES - (CONV_W - 1) + k
        xa = xa + cw_ref[k:k + 1, :] * xs_ref[off:off + ts, :]
    xs_ref[0:SUBLANES, :] = xs_ref[ts:ts + SUBLANES, :]

    sp = jax.nn.softplus(-lam_ref[...])
    xb = xa.astype(bf16)
    for n in range(RNN_BLOCKS):
        sl = slice(n * RNN_BLOCK, (n + 1) * RNN_BLOCK)
        gates = jnp.dot(xb[:, sl], wri_ref[n], preferred_element_type=f32)
        r = jax.nn.sigmoid(gates[:, :RNN_BLOCK] + br_ref[:, sl])
        ig = jax.nn.sigmoid(gates[:, RNN_BLOCK:] + bi_ref[:, sl])
        log_a = -LRU_C * r * sp[:, sl]
        a = jnp.exp(log_a)
        u = jnp.sqrt(-jnp.tanh(log_a) * (a * a + 1.0)) * (ig * xa[:, sl])
        a_ref[:, sl] = a
        u_ref[:, sl] = u

    def step8(j, h):
        base = pl.multiple_of(j * SUBLANES, SUBLANES)
        for k in range(SUBLANES):
            h = a_ref[pl.ds(base + k, 1), :] * h + u_ref[pl.ds(base + k, 1), :]
            u_ref[pl.ds(base + k, 1), :] = h
        return h

    h_ref[0:1, :] = lax.fori_loop(0, ts // SUBLANES, step8, h_ref[0:1, :])
    o_ref[...] = (u_ref[...] * jax.nn.gelu(zg_ref[...])).astype(bf16)


def _lru(z, B, S, cw, cb, wri, br, bi, lam):
    ts = min(512, S)
    nS = S // ts
    vec = lambda: pl.BlockSpec((1, D_RNN), lambda b, i: (0, 0))
    return pl.pallas_call(
        _lru_body,
        grid=(B, nS),
        in_specs=[pl.BlockSpec((ts, D_RNN), lambda b, i: (b * nS + i, C_RX // D_RNN)),
                  pl.BlockSpec((ts, D_RNN), lambda b, i: (b * nS + i, C_RG // D_RNN)),
                  pl.BlockSpec((CONV_W, D_RNN), lambda b, i: (0, 0)), vec(),
                  pl.BlockSpec((RNN_BLOCKS, RNN_BLOCK, 2 * RNN_BLOCK), lambda b, i: (0, 0, 0)),
                  vec(), vec(), vec()],
        out_specs=pl.BlockSpec((ts, D_RNN), lambda b, i: (b * nS + i, 0)),
        out_shape=jax.ShapeDtypeStruct((B * S, D_RNN), bf16),
        scratch_shapes=[pltpu.VMEM((ts + SUBLANES, D_RNN), f32), pltpu.VMEM((ts, D_RNN), f32),
                        pltpu.VMEM((ts, D_RNN), f32), pltpu.VMEM((SUBLANES, D_RNN), f32)],
        compiler_params=_params("parallel", "arbitrary"),
        name="rglru",
    )(z, z, cw, cb, wri, br, bi, lam)


def _prep_body(pos_ref, inv_ref, sgn_ref, zq_ref, zc_ref, zs_ref, zw_ref,
               qn_ref, qr_ref, kc_ref, vc_ref, ks_ref, vs_ref, kw_ref, vw_ref):
    ang = pos_ref[...].astype(f32) * inv_ref[...]
    cs = jnp.cos(ang)
    sn = jnp.sin(ang) * sgn_ref[...]

    def rope(v):
        return v * cs + pltpu.roll(v, HEAD_DIM // 2, axis=1) * sn

    scale = HEAD_DIM ** -0.5 * LOG2_E
    for h in range(N_HEADS):
        sl = slice(h * HEAD_DIM, (h + 1) * HEAD_DIM)
        q = zq_ref[:, sl]
        qn_ref[:, sl] = (q * scale).astype(bf16)
        qr_ref[:, sl] = (rope(q) * scale).astype(bf16)
    for g in range(N_KV):
        ksl = slice(g * HEAD_DIM, (g + 1) * HEAD_DIM)
        vsl = slice((N_KV + g) * HEAD_DIM, (N_KV + g + 1) * HEAD_DIM)
        kc_ref[g] = zc_ref[:, ksl].astype(bf16)
        vc_ref[g] = zc_ref[:, vsl].astype(bf16)
        ks_ref[g] = rope(zs_ref[:, ksl]).astype(bf16)
        vs_ref[g] = zs_ref[:, vsl].astype(bf16)
        kw_ref[g] = rope(zw_ref[:, ksl]).astype(bf16)
        vw_ref[g] = zw_ref[:, vsl].astype(bf16)


def _prep(z, pos, inv2, sgn, B, S):
    T = B * S
    tp = min(256, S)
    nS = S // tp
    kvw = 2 * N_KV * HEAD_DIM
    row = lambda b, i: b * nS + i
    kv_out = pl.BlockSpec((None, N_KV, tp, HEAD_DIM), lambda b, i: (b, 0, i, 0))
    kv_shape = jax.ShapeDtypeStruct((B, N_KV, S, HEAD_DIM), bf16)
    q_shape = jax.ShapeDtypeStruct((T, D_ATTN), bf16)
    return pl.pallas_call(
        _prep_body,
        grid=(B, nS),
        in_specs=[pl.BlockSpec((tp, 1), lambda b, i: (row(b, i), 0)),
                  pl.BlockSpec((1, HEAD_DIM), lambda b, i: (0, 0)),
                  pl.BlockSpec((1, HEAD_DIM), lambda b, i: (0, 0)),
                  pl.BlockSpec((tp, D_ATTN), lambda b, i: (row(b, i), C_Q // D_ATTN)),
                  pl.BlockSpec((tp, kvw), lambda b, i: (row(b, i), C_KV // kvw)),
                  pl.BlockSpec((tp, kvw), lambda b, i: (row(b, i), C_KV // kvw + 1)),
                  pl.BlockSpec((tp, kvw), lambda b, i: (row(b, i), C_KV // kvw + 2))],
        out_specs=[pl.BlockSpec((tp, D_ATTN), lambda b, i: (row(b, i), 0)),
                   pl.BlockSpec((tp, D_ATTN), lambda b, i: (row(b, i), 0))] + [kv_out] * 6,
        out_shape=[q_shape, q_shape] + [kv_shape] * 6,
        compiler_params=_params("parallel", "parallel"),
        name="attn_prep",
    )(pos, inv2, sgn, z, z, z, z)


def _cmp_body(kf_ref, vf_ref, w1k_ref, pek_ref, b1k_ref, w2k_ref, b2k_ref,
              w1v_ref, pev_ref, b1v_ref, w2v_ref, b2v_ref, kc_ref, vc_ref):
    nc = kf_ref.shape[0]
    half = CMP_STRIDE * HEAD_DIM

    def mlp(f_ref, w1_ref, pe_ref, b1_ref, w2_ref, b2_ref):
        f = f_ref[...]
        first = jnp.dot(f, w1_ref[0:half, :], preferred_element_type=f32)
        second = jnp.dot(f, w1_ref[half:2 * half, :], preferred_element_type=f32)
        pew = jnp.dot(pe_ref[...], w1_ref[...], preferred_element_type=f32)[0:1]
        hid = jax.nn.gelu(first + pltpu.roll(second, nc - 1, axis=0) + pew + b1_ref[...])
        return jnp.dot(hid.astype(bf16), w2_ref[...], preferred_element_type=f32) + b2_ref[...]

    kc_ref[...] = mlp(kf_ref, w1k_ref, pek_ref, b1k_ref, w2k_ref, b2k_ref)
    vc_ref[...] = mlp(vf_ref, w1v_ref, pev_ref, b1v_ref, w2v_ref, b2v_ref)


def _compress(kflat, vflat, pk, pv):
    B, G, NC, F = kflat.shape
    fl = CMP_LEN * HEAD_DIM
    flat = pl.BlockSpec((None, None, NC, F), lambda b, g: (b, g, 0, 0))
    full = lambda shape: pl.BlockSpec(shape, lambda b, g: (0,) * len(shape))
    wspecs = [full((fl, CMP_HID)), full((SUBLANES, fl)), full((1, CMP_HID)),
              full((CMP_HID, HEAD_DIM)), full((1, HEAD_DIM))]
    out = pl.BlockSpec((None, None, NC, HEAD_DIM), lambda b, g: (b, g, 0, 0))
    shape = jax.ShapeDtypeStruct((B, G, NC, HEAD_DIM), f32)
    return pl.pallas_call(
        _cmp_body,
        grid=(B, G),
        in_specs=[flat, flat] + wspecs + wspecs,
        out_specs=[out, out],
        out_shape=[shape, shape],
        compiler_params=_params("parallel", "parallel"),
        name="compress",
    )(kflat, vflat, *pk, *pv)


def _softmax0(s):
    e = jnp.exp2(s - jnp.max(s, axis=0, keepdims=True))
    return e * (1.0 / jnp.sum(e, axis=0, keepdims=True))


def _nsa_body(qn_ref, qr_ref, gz_ref, kc_ref, vc_ref, ks_ref, vs_ref, kw_ref, vw_ref, ovt_ref,
              o_ref, sel_ref, gt_ref, qrs_ref, part_ref, m_ref, l_ref, acc_ref, sa_ref, sb_ref,
              *, S, tq, top_n, ck):
    g = pl.program_id(1)
    t0 = pl.program_id(2) * tq
    NC, NS = S // CMP_STRIDE, S // SEL_BLOCK
    heads = [slice(h * HEAD_DIM, (h + 1) * HEAD_DIM) for h in range(HPG)]
    trow = t0 + lax.broadcasted_iota(i32, (1, tq), 1)

    tile4 = lambda a: jnp.concatenate([a] * HPG, axis=1)
    qn = jnp.concatenate([qn_ref[:, hs] for hs in heads], axis=0)
    qrs_ref[...] = jnp.concatenate([qr_ref[:, hs] for hs in heads], axis=0)

    gt_ref[...] = jax.nn.sigmoid(gz_ref[...]).T

    def gate(branch):
        return jnp.concatenate([gt_ref[pl.ds(g * (HPG * 3) + h * 3 + branch, 1), :] for h in range(HPG)], axis=1)

    cmp_end = lax.broadcasted_iota(i32, (NC, tq), 0) * CMP_STRIDE + (CMP_LEN - 1)
    cbias = jnp.where(cmp_end <= trow, 0.0, NEG_INF)
    live = (trow >= CMP_LEN - 1).astype(f32)
    sc = lax.dot_general(kc_ref[...].astype(bf16), qn, NT_DIMS, preferred_element_type=f32)
    pc = _softmax0(sc + tile4(cbias)) * tile4(live)
    ocT = lax.dot_general(vc_ref[...].astype(bf16), pc.astype(bf16), TN_DIMS, preferred_element_type=f32)
    ps = pc[:, 0:tq]
    for h in range(1, HPG):
        ps = ps + pc[:, h * tq:(h + 1) * tq]

    wl = WINDOW + tq
    start = pl.multiple_of(jnp.maximum(t0 - WINDOW, 0), tq)
    dist = trow - (start + lax.broadcasted_iota(i32, (wl, tq), 0))
    wbias = jnp.where(dist >= 0, jnp.where(dist < WINDOW, 0.0, NEG_INF), NEG_INF)
    sw = lax.dot_general(kw_ref[pl.ds(start, wl), :], qrs_ref[...], NT_DIMS, preferred_element_type=f32)
    pw = _softmax0(sw + tile4(wbias)).astype(bf16)
    owT = lax.dot_general(vw_ref[pl.ds(start, wl), :], pw, TN_DIMS, preferred_element_type=f32)
    part_ref[...] = gate(0) * ocT + gate(2) * owT

    ps_hi, ps_lo = _split_bf16(ps)
    imp = (jnp.dot(ovt_ref[...], ps_hi, preferred_element_type=f32)
           + jnp.dot(ovt_ref[...], ps_lo, preferred_element_type=f32))
    j = lax.broadcasted_iota(i32, (NS, tq), 0)
    t = t0 + lax.broadcasted_iota(i32, (NS, tq), 1)
    cur = t >> (SEL_BLOCK.bit_length() - 1)
    forced = (j == 0) | (j == cur) | (j == cur - 1)
    imp = jnp.where(forced, FORCE_SCORE, imp)
    imp = jnp.where(j * SEL_BLOCK <= t, imp, NEG_INF)
    sel = jnp.zeros((NS, tq), f32)
    for _ in range(top_n):
        m = jnp.max(imp, axis=0, keepdims=True)
        first = jnp.min(jnp.where(imp == m, j, NS), axis=0, keepdims=True)
        hit = j == first
        sel = jnp.where(hit & (m > 0.5 * NEG_INF), 1.0, sel)
        imp = jnp.where(hit, LOWEST, imp)
    sel_ref[...] = sel

    r64 = lax.broadcasted_iota(i32, (SEL_BLOCK, tq), 0)
    bpc = ck // SEL_BLOCK

    nch = (t0 + tq + ck - 1) // ck
    last = S // ck - 1

    def scores(c):
        base = pl.multiple_of(jnp.minimum(c, last) * ck, ck)
        return lax.dot_general(ks_ref[pl.ds(base, ck), :], qrs_ref[...], NT_DIMS, preferred_element_type=f32)

    def attend(c, s_ref):
        base = pl.multiple_of(c * ck, ck)
        bias = []
        for r in range(bpc):
            srow = sel_ref[pl.ds(c * bpc + r, 1), :]
            lim = jnp.where(srow > 0.5, trow, -1) - (base + SEL_BLOCK * r)
            bias.append(jnp.where(r64 <= lim, 0.0, NEG_INF))
        s = s_ref[...] + tile4(jnp.concatenate(bias, axis=0))
        m = m_ref[0:1, :]
        m_new = jnp.maximum(m, jnp.max(s, axis=0, keepdims=True))
        alpha = jnp.exp2(m - m_new)
        p = jnp.exp2(s - m_new)
        l_ref[0:1, :] = alpha * l_ref[0:1, :] + jnp.sum(p, axis=0, keepdims=True)
        pv = lax.dot_general(vs_ref[pl.ds(base, ck), :], p.astype(bf16), TN_DIMS, preferred_element_type=f32)
        acc_ref[...] = alpha * acc_ref[...] + pv
        m_ref[0:1, :] = m_new

    m_ref[...] = jnp.full_like(m_ref, NEG_INF)
    l_ref[...] = jnp.zeros_like(l_ref)
    acc_ref[...] = jnp.zeros_like(acc_ref)
    sa_ref[...] = scores(0)

    def pair(i, _):
        sb_ref[...] = scores(2 * i + 1)
        attend(2 * i, sa_ref)
        sa_ref[...] = scores(2 * i + 2)
        attend(2 * i + 1, sb_ref)
        return 0

    lax.fori_loop(0, nch // 2, pair, 0)

    @pl.when(nch % 2 == 1)
    def _():
        attend(nch - 1, sa_ref)

    oT = part_ref[...] + gate(1) * (acc_ref[...] * (1.0 / l_ref[0:1, :]))
    for h, hs in enumerate(heads):
        o_ref[:, hs] = oT[:, h * tq:(h + 1) * tq].T.astype(bf16)


def _nsa(qn, qr, z, kc, vc, ks, vs, kw, vw, ovt, B, S):
    tq, ck = 128, 512
    assert S % ck == 0 and S >= WINDOW + tq
    nS = S // tq
    NC, NS = S // CMP_STRIDE, S // SEL_BLOCK
    gw = HPG * HEAD_DIM
    qspec = pl.BlockSpec((tq, gw), lambda b, g, i: (b * nS + i, g))
    seq = lambda n: pl.BlockSpec((None, None, n, HEAD_DIM), lambda b, g, i: (b, g, 0, 0))
    body = functools.partial(_nsa_body, S=S, tq=tq, top_n=min(SEL_TOPK, NS), ck=ck)
    return pl.pallas_call(
        body,
        grid=(B, N_KV, nS),
        in_specs=[qspec, qspec,
                  pl.BlockSpec((tq, LANES), lambda b, g, i: (b * nS + i, C_NG // LANES)),
                  seq(NC), seq(NC), seq(S), seq(S), seq(S), seq(S),
                  pl.BlockSpec((NS, NC), lambda b, g, i: (0, 0))],
        out_specs=pl.BlockSpec((tq, gw), lambda b, g, i: (b * nS + i, g)),
        out_shape=jax.ShapeDtypeStruct((B * S, D_ATTN), bf16),
        scratch_shapes=[pltpu.VMEM((NS, tq), f32), pltpu.VMEM((LANES, tq), f32),
                        pltpu.VMEM((HPG * tq, HEAD_DIM), bf16), pltpu.VMEM((HEAD_DIM, HPG * tq), f32),
                        pltpu.VMEM((SUBLANES, HPG * tq), f32), pltpu.VMEM((SUBLANES, HPG * tq), f32),
                        pltpu.VMEM((HEAD_DIM, HPG * tq), f32),
                        pltpu.VMEM((ck, HPG * tq), f32), pltpu.VMEM((ck, HPG * tq), f32)],
        compiler_params=_params("parallel", "parallel", "arbitrary"),
        name="nsa",
    )(qn, qr, z, kc, vc, ks, vs, kw, vw, ovt)


def _extract_max(x, tag, invalid):
    m = jnp.max(x, axis=0, keepdims=True)
    first = jnp.min(jnp.where(x == m, tag, invalid), axis=0, keepdims=True)
    return m, first


def _oddeven_mergesort_pairs(n):
    pairs = []
    p = 1
    while p < n:
        k = p
        while k >= 1:
            for j in range(k % p, n - k, 2 * k):
                for i in range(min(k, n - j - k)):
                    if (i + j) // (2 * p) == (i + j + k) // (2 * p):
                        pairs.append((i + j, i + j + k))
            k //= 2
        p *= 2
    return pairs


def _compare_exchange(v, p, i, j):
    swap = v[j] > v[i]
    v[i], v[j] = jnp.where(swap, v[j], v[i]), jnp.where(swap, v[i], v[j])
    p[i], p[j] = jnp.where(swap, p[j], p[i]), jnp.where(swap, p[i], p[j])


def _sorted_top16(v, p):
    n = PEER_TOPK
    v, p = list(v), list(p)
    for i, j in _oddeven_mergesort_pairs(n):
        _compare_exchange(v, p, i, j)
    dropped = jnp.full_like(v[0], LOWEST)
    for shift in (4, 2, 1):
        bv = [pltpu.roll(a, shift, axis=0) for a in v]
        bp = [pltpu.roll(a, shift, axis=0) for a in p]
        dropped = jnp.maximum(dropped, pltpu.roll(dropped, shift, axis=0))
        for r in range(n):
            take = bv[n - 1 - r] > v[r]
            dropped = jnp.maximum(dropped, jnp.where(take, v[r], bv[n - 1 - r]))
            v[r] = jnp.where(take, bv[n - 1 - r], v[r])
            p[r] = jnp.where(take, bp[n - 1 - r], p[r])
        d = n // 2
        while d >= 1:
            for i in range(n):
                if i & d == 0:
                    _compare_exchange(v, p, i, i + d)
            d //= 2
    tie = v[n - 1] == dropped
    for k in range(n - 1):
        tie = tie | (v[k] == v[k + 1])
    tie = jnp.where(tie, 1.0, 0.0)[0:1, :]
    return [a[0:1, :] for a in v], [a[0:1, :] for a in p], tie


def _post_body(ha_ref, ob_ref, zm0_ref, zm1_ref, x_ref, wa_ref, wb_ref, wo_ref, gf_ref, wq_ref, kbd_ref,
               h_ref, n_ref, e_ref, g_ref, st_ref, va_ref, ia_ref, et_ref, gt_ref):
    tm = x_ref.shape[0]
    K = PEER_TOPK
    ya = jnp.dot(ha_ref[...], wa_ref[...], preferred_element_type=f32)
    yb = jnp.dot(ob_ref[...], wb_ref[...], preferred_element_type=f32)
    merged = jax.nn.sigmoid(zm0_ref[...]) * ya + jax.nn.sigmoid(zm1_ref[...]) * yb
    h = x_ref[...] + jnp.dot(merged.astype(bf16), wo_ref[...], preferred_element_type=f32)
    h_ref[...] = h
    n = _rms(h, gf_ref[...]).astype(bf16)
    n_ref[...] = n
    q = jnp.dot(n, wq_ref[...], preferred_element_type=f32).astype(bf16)
    st_ref[...] = lax.dot_general(kbd_ref[...], q, NT_DIMS, preferred_element_type=f32)

    ncols = tm // LANES
    rows = lax.broadcasted_iota(i32, (N_KEYS, LANES), 0)
    r8 = lax.broadcasted_iota(i32, (SUBLANES, LANES), 0)

    def top_half(hd, _):
        chains = [(2 * hd + half, c) for half in range(2) for c in range(ncols)]
        scores = lambda hc, c: st_ref[pl.ds(pl.multiple_of(hc * N_KEYS, N_KEYS), N_KEYS), c * LANES:(c + 1) * LANES]
        tied = jnp.zeros((1, LANES), f32)
        for hc, c in chains:
            x = scores(hc, c)
            vals, idxs, tie = _sorted_top16([x[r * SUBLANES:(r + 1) * SUBLANES, :] for r in range(K)],
                                            [r8 + r * SUBLANES for r in range(K)])
            for k in range(K):
                va_ref[c, pl.ds(hc * K + k, 1), :] = vals[k]
                ia_ref[c, pl.ds(hc * K + k, 1), :] = idxs[k]
            tied = jnp.maximum(tied, tie)

        @pl.when(jnp.max(tied) > 0.0)
        def _():
            ys = [scores(hc, c) for hc, c in chains]
            for k in range(K):
                for n, (hc, c) in enumerate(chains):
                    m, first = _extract_max(ys[n], rows, N_KEYS)
                    va_ref[c, pl.ds(hc * K + k, 1), :] = m
                    ia_ref[c, pl.ds(hc * K + k, 1), :] = first
                    ys[n] = jnp.where(rows == first, LOWEST, ys[n])

        return 0

    lax.fori_loop(0, PEER_HEADS, top_half, 0)

    def combine(hd, _):
        for c in range(ncols):
            combine_block(hd, c)
        return 0

    def combine_block(hd, c):
        b1 = pl.multiple_of(hd * 2 * K, 2 * K)
        v1, v2 = va_ref[c, pl.ds(b1, K), :], va_ref[c, pl.ds(b1 + K, K), :]
        i1, i2 = ia_ref[c, pl.ds(b1, K), :], ia_ref[c, pl.ds(b1 + K, K), :]
        cand = [v1[0:1] + v2]
        tag = [lax.broadcasted_iota(i32, (K, LANES), 0)]
        eid = [(i1[0:1] * N_KEYS + i2) * EXPERT_ROWS]
        for a in range(1, SUBLANES):
            ok = r8 < K // (a + 1)
            cand.append(jnp.where(ok, v1[a:a + 1] + v2[0:SUBLANES], LOWEST))
            tag.append(a * K + r8)
            eid.append((i1[a:a + 1] * N_KEYS + i2[0:SUBLANES]) * EXPERT_ROWS)
        cand.append(v1[SUBLANES:K] + v2[0:1])
        tag.append((SUBLANES + r8) * K)
        eid.append((i1[SUBLANES:K] * N_KEYS + i2[0:1]) * EXPERT_ROWS)
        cand, tag, eid = (jnp.concatenate(v, axis=0) for v in (cand, tag, eid))
        tag = tag * TAG_SHIFT + eid
        best0 = None
        den = jnp.zeros((1, LANES), f32)
        for k in range(K):
            m, first = _extract_max(cand, tag, K * K * TAG_SHIFT)
            hit = tag == first
            et_ref[c, pl.ds(hd * K + k, 1), :] = first & (TAG_SHIFT - 1)
            cand = jnp.where(hit, LOWEST, cand)
            best0 = m if best0 is None else best0
            ex = jnp.exp(m - best0)
            gt_ref[c, pl.ds(hd * K + k, 1), :] = ex
            den = den + ex
        gt_ref[c, pl.ds(hd * K, K), :] = gt_ref[c, pl.ds(hd * K, K), :] * (1.0 / den)

    lax.fori_loop(0, PEER_HEADS, combine, 0)
    for c in range(ncols):
        e_ref[c * LANES:(c + 1) * LANES, :] = et_ref[c].T
        g_ref[c * LANES:(c + 1) * LANES, :] = gt_ref[c].T


def _post(hag, ob, z, x2, wa, wb, wo, gf, wq, kbd):
    T, D = x2.shape
    tm = min(512, T)
    HK = PEER_HEADS * PEER_TOPK
    rowblk = lambda w, c=0: pl.BlockSpec((tm, w), lambda i: (i, c))
    full = lambda shape: pl.BlockSpec(shape, lambda i: (0,) * len(shape), pipeline_mode=pl.Buffered(1))
    return pl.pallas_call(
        _post_body,
        grid=(T // tm,),
        in_specs=[rowblk(D), rowblk(D), rowblk(D, C_M // D), rowblk(D, C_M // D + 1), rowblk(D),
                  full((D, D)), full((D, D)), full((D, D)), full((1, D)), full((D, D)),
                  full((2 * PEER_HEADS * N_KEYS, D))],
        out_specs=[rowblk(D), rowblk(D), rowblk(HK), rowblk(HK)],
        out_shape=[jax.ShapeDtypeStruct((T, D), f32), jax.ShapeDtypeStruct((T, D), bf16),
                   jax.ShapeDtypeStruct((T, HK), i32), jax.ShapeDtypeStruct((T, HK), f32)],
        scratch_shapes=[pltpu.VMEM((2 * PEER_HEADS * N_KEYS, tm), f32),
                        pltpu.VMEM((tm // LANES, 2 * PEER_HEADS * PEER_TOPK, LANES), f32),
                        pltpu.VMEM((tm // LANES, 2 * PEER_HEADS * PEER_TOPK, LANES), i32),
                        pltpu.VMEM((tm // LANES, HK, LANES), i32), pltpu.VMEM((tm // LANES, HK, LANES), f32)],
        compiler_params=_params("parallel"),
        name="post_route",
    )(hag, ob, z, z, x2, wa, wb, wo, gf, wq, kbd)


PEER_GROUP = 64
EXPERT_ROWS = SUBLANES // 2
TAG_SHIFT = N_KEYS * N_KEYS * EXPERT_ROWS


def _diag_mask(width):
    lane = lax.broadcasted_iota(i32, (SUBLANES, width), 1)
    sub = lax.broadcasted_iota(i32, (SUBLANES, width), 0)
    return (lane & (SUBLANES - 1)) == sub


def _gather_tiles(idx_ref, tab_ref, t):
    tiles = []
    for j in range(idx_ref.shape[1]):
        r = pl.multiple_of(idx_ref[t, j], EXPERT_ROWS)
        tiles.append(pltpu.bitcast(tab_ref[pl.ds(r, EXPERT_ROWS), :], bf16))
    return jnp.concatenate(tiles, axis=0)


def _peer_u_body(idx_ref, x_ref, g_ref, tab_ref, fold_ref, act_ref, y_ref):
    tt, hk = g_ref.shape
    diag = _diag_mask(hk * SUBLANES)

    def group(gi, _):
        for k in range(PEER_GROUP):
            t = gi * PEER_GROUP + k
            y = lax.dot_general(x_ref[t], _gather_tiles(idx_ref, tab_ref, t), NT_DIMS, preferred_element_type=f32)
            y_ref[pl.ds(t, 1), :] = jnp.sum(jnp.where(diag, y, 0.0), axis=0, keepdims=True)
        return 0

    lax.fori_loop(0, tt // PEER_GROUP, group, 0)
    y_hi, y_lo = _split_bf16(y_ref[...])
    s = (jnp.dot(y_hi, fold_ref[...], preferred_element_type=f32)
         + jnp.dot(y_lo, fold_ref[...], preferred_element_type=f32))
    act_ref[...] = jax.nn.gelu(s) * g_ref[...]


def _peer_v_body(idx_ref, act_ref, h_ref, tab_ref, spread_ref, o_ref, a_ref):
    tt, hk = act_ref.shape
    diag = _diag_mask(hk * SUBLANES)
    a_hi, a_lo = _split_bf16(act_ref[...])
    a_ref[...] = (jnp.dot(a_hi, spread_ref[...], preferred_element_type=f32)
                  + jnp.dot(a_lo, spread_ref[...], preferred_element_type=f32))

    def group(gi, _):
        for k in range(PEER_GROUP):
            t = gi * PEER_GROUP + k
            a8 = jnp.where(diag, jnp.broadcast_to(a_ref[pl.ds(t, 1), :], diag.shape), 0.0)
            lhs = jnp.concatenate(_split_bf16(a8), axis=0)
            out = jnp.dot(lhs, _gather_tiles(idx_ref, tab_ref, t), preferred_element_type=f32)
            o_ref[t] = h_ref[t] + out[0:SUBLANES] + out[SUBLANES:2 * SUBLANES]
        return 0

    lax.fori_loop(0, tt // PEER_GROUP, group, 0)


def _peer(n3, e_idx, gates, h3, u_tiles, v_tiles, fold, spread):
    T = n3.shape[0]
    HK = e_idx.shape[1]
    tt = min(256, T)
    smem_idx = pl.BlockSpec((tt, HK), lambda i: (i, 0), memory_space=pltpu.SMEM)
    tok = pl.BlockSpec((tt, SUBLANES, LANES), lambda i: (i, 0, 0))
    row = pl.BlockSpec((tt, HK), lambda i: (i, 0))
    table = pl.BlockSpec(memory_space=pltpu.VMEM)
    act = pl.pallas_call(
        _peer_u_body,
        grid=(T // tt,),
        in_specs=[smem_idx, tok, row, table, pl.BlockSpec(fold.shape, lambda i: (0, 0))],
        out_specs=row,
        out_shape=jax.ShapeDtypeStruct((T, HK), f32),
        scratch_shapes=[pltpu.VMEM((tt, HK * SUBLANES), f32)],
        compiler_params=_params("parallel"),
        name="peer_u",
    )(e_idx, n3, gates, u_tiles, fold)
    return pl.pallas_call(
        _peer_v_body,
        grid=(T // tt,),
        in_specs=[smem_idx, row, tok, table, pl.BlockSpec(spread.shape, lambda i: (0, 0))],
        out_specs=tok,
        out_shape=jax.ShapeDtypeStruct(h3.shape, f32),
        scratch_shapes=[pltpu.VMEM((tt, HK * SUBLANES), f32)],
        compiler_params=_params("parallel"),
        name="peer_v",
    )(e_idx, act, h3, v_tiles, spread)


def _ple_body(h_ref, p_ref, gp_ref, wg_ref, wp_ref, gn_ref, o_ref, *, last):
    h = h_ref[...]
    gate = jax.nn.sigmoid(jnp.dot(_rms(h, gp_ref[...]).astype(bf16), wg_ref[...], preferred_element_type=f32))
    h = h + gate * jnp.dot(p_ref[...].astype(bf16), wp_ref[...], preferred_element_type=f32)
    o_ref[...] = _rms(h, gn_ref[...]) if last else h


def _ple(h2, p2, gp, wg, wp, gn, last):
    T, D = h2.shape
    P = p2.shape[1]
    tm = min(512, T)
    full = lambda shape: pl.BlockSpec(shape, lambda i: (0,) * len(shape))
    return pl.pallas_call(
        functools.partial(_ple_body, last=last),
        grid=(T // tm,),
        in_specs=[pl.BlockSpec((tm, D), lambda i: (i, 0)), pl.BlockSpec((tm, P), lambda i: (i, 0)),
                  full((1, D)), full((D, D)), full((P, D)), full((1, D))],
        out_specs=pl.BlockSpec((tm, D), lambda i: (i, 0)),
        out_shape=jax.ShapeDtypeStruct((T, D), f32),
        compiler_params=_params("parallel"),
        name="ple_norm",
    )(h2, p2, gp, wg, wp, gn)


def _overlap_t(S):
    NC, NS = S // CMP_STRIDE, S // SEL_BLOCK
    n = np.arange(NC)[None, :] * CMP_STRIDE
    j = np.arange(NS)[:, None] * SEL_BLOCK
    ov = (n < j + SEL_BLOCK) & (n + CMP_LEN - 1 >= j) & (np.arange(NC)[None, :] < NC - 1)
    return jnp.asarray(ov, bf16)


def _pack_expert_tiles(tab):
    even = jnp.concatenate([tab[:, (2 * s) * LANES:(2 * s + 1) * LANES] for s in range(EXPERT_ROWS)], axis=1)
    odd = jnp.concatenate([tab[:, (2 * s + 1) * LANES:(2 * s + 2) * LANES] for s in range(EXPERT_ROWS)], axis=1)
    bits = lambda a: lax.bitcast_convert_type(a.astype(bf16), jnp.uint16).astype(jnp.uint32)
    words = bits(even) | (bits(odd) << 16)
    return lax.bitcast_convert_type(words, i32).reshape(tab.shape[0] * EXPERT_ROWS, LANES)


def _fold_matrix(hk):
    return jnp.asarray(np.arange(hk * SUBLANES)[:, None] // SUBLANES == np.arange(hk)[None, :], bf16)


def kernel(x, p, positions, norm_mix, w_in, conv_w, conv_b, lru_wr, lru_br, lru_wi, lru_bi, lru_lam,
           cmp_pe_k, cmp_w1_k, cmp_b1_k, cmp_w2_k, cmp_b2_k, cmp_pe_v, cmp_w1_v, cmp_b1_v, cmp_w2_v, cmp_b2_v,
           w_a, w_b, w_out, norm_ffn, peer_wq, peer_keys, peer_u, peer_v, norm_ple, ple_wg, ple_wp,
           norm_final):
    B, S, D = x.shape
    T = B * S
    depth = norm_mix.shape[0]
    HK = PEER_HEADS * PEER_TOPK
    half = HEAD_DIM // 2
    inv = ROPE_THETA ** (-jnp.arange(half, dtype=f32) / half)
    inv2 = jnp.concatenate([inv, inv])[None, :]
    sgn = jnp.concatenate([-jnp.ones((half,), f32), jnp.ones((half,), f32)])[None, :]
    pos = positions.reshape(T, 1)
    ovt = _overlap_t(S)
    fold = _fold_matrix(HK)
    spread = fold.T
    fl = CMP_LEN * HEAD_DIM
    row = lambda v: v.reshape(1, -1)

    h = x.reshape(T, D)
    for i in range(depth):
        w = w_in[i]
        sp = np.cumsum((D_RNN, D_RNN, D_ATTN, 6 * N_KV * HEAD_DIM, 3 * N_HEADS, 2 * D))
        w_ng = jnp.pad(w[:, sp[3]:sp[4]], ((0, 0), (0, NZ - C_NG - 3 * N_HEADS)))
        w_all = jnp.concatenate([w[:, :sp[2]], w[:, sp[4]:], w[:, sp[2]:sp[3]], w_ng], axis=1).astype(bf16)
        z = _inproj(h, row(norm_mix[i]), w_all)

        wri = jnp.concatenate([lru_wr[i], lru_wi[i]], axis=-1).astype(bf16)
        hag = _lru(z, B, S, conv_w[i], row(conv_b[i]), wri, row(lru_br[i]), row(lru_bi[i]), row(lru_lam[i]))

        qn, qr, kc_in, vc_in, ks, vs, kw, vw = _prep(z, pos, inv2, sgn, B, S)
        flat = lambda a: a.reshape(B, N_KV, S // CMP_STRIDE, CMP_STRIDE * HEAD_DIM)
        cmp_params = lambda pe, w1, b1, w2, b2: (
            w1.astype(bf16), jnp.broadcast_to(pe.reshape(1, fl), (SUBLANES, fl)).astype(bf16), row(b1),
            w2.astype(bf16), row(b2))
        kc, vc = _compress(flat(kc_in), flat(vc_in),
                           cmp_params(cmp_pe_k[i], cmp_w1_k[i], cmp_b1_k[i], cmp_w2_k[i], cmp_b2_k[i]),
                           cmp_params(cmp_pe_v[i], cmp_w1_v[i], cmp_b1_v[i], cmp_w2_v[i], cmp_b2_v[i]))
        ob = _nsa(qn, qr, z, kc, vc, ks, vs, kw, vw, ovt, B, S)

        keys = peer_keys[i]
        same_head = jnp.eye(PEER_HEADS, dtype=bool)[:, None, None, :, None, None]
        same_half = jnp.eye(2, dtype=bool)[None, :, None, None, :, None]
        kbd = jnp.where(same_head & same_half, keys[None, :, :, None, None, :], 0.0)
        kbd = kbd.reshape(2 * PEER_HEADS * N_KEYS, PEER_HEADS * PEER_QDIM).astype(bf16)
        h1, n2, e_idx, gates = _post(hag, ob, z, h, w_a[i].astype(bf16), w_b[i].astype(bf16),
                                     w_out[i].astype(bf16), row(norm_ffn[i]), peer_wq[i].astype(bf16), kbd)

        tiles = _pack_expert_tiles
        h2 = _peer(n2.reshape(T, SUBLANES, LANES), e_idx, gates, h1.reshape(T, SUBLANES, LANES),
                   tiles(peer_u[i]), tiles(peer_v[i]), fold, spread)

        h = _ple(h2.reshape(T, D), p[i].reshape(T, -1), row(norm_ple[i]), ple_wg[i].astype(bf16),
                 ple_wp[i].astype(bf16), row(norm_final), last=(i == depth - 1))
    return h.reshape(B, S, D)
```
